```python
import math
import jax, jax.numpy as jnp
from jax import lax
import numpy as np

D_MODEL = 2048
BATCH = 32
SEQ = 256
DEPTH = 2
DEC_BATCH = 8
DEC_SEQ = 1024
PAST_LEN = 256

GRID_W = 64
N_GROUPS = 4
GROUP_W = D_MODEL // N_GROUPS
GLA_HEADS = 4
GLA_DV = GROUP_W // GLA_HEADS
GLA_DK = GLA_DV // 2
GLA_LR = 16
GLA_NORMALIZER = 16.0
GLA_CHUNK = 64
MLA_HEADS = 4
MLA_NOPE = 128
MLA_ROPE = 64
MLA_V = GROUP_W // MLA_HEADS
MLA_QK = MLA_NOPE + MLA_ROPE
MLA_Q_RANK = 3 * D_MODEL // 16
MLA_KV_RANK = D_MODEL // 16
DIFF_HEADS = 4
DIFF_DV = GROUP_W // DIFF_HEADS
DIFF_DQK = DIFF_DV // 2
POOL_WINDOWS = (2, 4, 8, 16)
POOL_CH = GROUP_W // len(POOL_WINDOWS)
PEER_HEADS = 8
PEER_NKEYS = 128
PEER_EXPERTS = PEER_NKEYS * PEER_NKEYS
PEER_TOPK = 16
PEER_DKEY = 256
PEER_DHALF = PEER_DKEY // 2
PEER_BLOCK = 128
Q_BLOCK = 128
ROPE_BASE = 10000.0
EPS = 1e-6
IN_SPLITS = (GLA_HEADS * GLA_DK, GLA_HEADS * GLA_DK, GLA_HEADS * GLA_DV, GLA_HEADS * GLA_DV, 2 * GLA_LR,
             MLA_Q_RANK, MLA_KV_RANK, MLA_ROPE,
             DIFF_HEADS * 2 * DIFF_DQK, DIFF_HEADS * 2 * DIFF_DQK, DIFF_HEADS * DIFF_DV,
             GROUP_W)
IN_COLS = sum(IN_SPLITS)

kernel_name = 'hybrid_diffusion_prefix_step'


def rms_norm(x, g):
    xf = x.astype(jnp.float32)
    y = xf * lax.rsqrt(jnp.mean(xf * xf, axis=-1, keepdims=True) + EPS)
    return (y * g.astype(jnp.float32)).astype(x.dtype)


def grid_positions(rows):
    row = jnp.repeat(jnp.arange(rows, dtype=jnp.float32), GRID_W)
    col = (jnp.arange(rows * GRID_W) % GRID_W).astype(jnp.float32)
    return row, col


def rope_axis(x, pos):
    half = x.shape[-1] // 2
    freq = ROPE_BASE ** (-jnp.arange(half, dtype=jnp.float32) / half)
    ang = pos[:, None] * freq[None, :]
    shape = (1, pos.shape[0]) + (1,) * (x.ndim - 3) + (half,)
    cos = jnp.cos(ang).reshape(shape)
    sin = jnp.sin(ang).reshape(shape)
    xf = x.astype(jnp.float32)
    x1, x2 = xf[..., :half], xf[..., half:]
    return jnp.concatenate([x1 * cos - x2 * sin, x1 * sin + x2 * cos], axis=-1).astype(x.dtype)


def axial_rope(x, row, col):
    h = x.shape[-1] // 2
    return jnp.concatenate([rope_axis(x[..., :h], row), rope_axis(x[..., h:], col)], axis=-1)


def query_blocks(fn, *qs):
    b, s = qs[0].shape[:2]
    blk = min(Q_BLOCK, s)
    nb = s // blk
    stacked = tuple(jnp.moveaxis(q.reshape((b, nb, blk) + q.shape[2:]), 1, 0) for q in qs)
    out = lax.map(lambda a: fn(*a), stacked)
    out = jnp.moveaxis(out, 0, 1)
    return out.reshape((b, s) + out.shape[3:])


def gla_chunk_scan(q, k, v, logg, s0):
    b, s, h, _ = q.shape
    dv = v.shape[-1]
    c = min(GLA_CHUNK, s)
    n = s // c
    lower = jnp.tril(jnp.ones((c, c), dtype=bool))[None, :, :, None, None]

    def chunks(a):
        return jnp.moveaxis(a.reshape(b, n, c, h, a.shape[-1]), 1, 0)

    def step(state, inp):
        qc, kc, vc, gc = inp
        cum = jnp.cumsum(gc, axis=1)
        o_inter = jnp.einsum('bihk,bhkv->bihv', qc * jnp.exp(cum), state)
        decay = jnp.exp(jnp.where(lower, cum[:, :, None] - cum[:, None, :], -jnp.inf))
        att = jnp.einsum('bijhk,bjhk->bhij', qc[:, :, None] * decay, kc)
        o_intra = jnp.einsum('bhij,bjhv->bihv', att, vc)
        last = cum[:, -1]
        new_state = jnp.exp(last)[..., None] * state + jnp.einsum(
            'bjhk,bjhv->bhkv', kc * jnp.exp(last[:, None] - cum), vc)
        return new_state, o_inter + o_intra

    s_fin, o = lax.scan(step, s0, (chunks(q), chunks(k), chunks(v), chunks(logg)))
    return jnp.moveaxis(o, 0, 1).reshape(b, s, h, dv), s_fin


def gla_mixer(q, k, v, gate, lr, gk_w, gk_b, onorm_g, s0):
    b, s, _ = q.shape
    f32 = jnp.float32
    qh = q.astype(f32).reshape(b, s, GLA_HEADS, GLA_DK) * GLA_DK ** -0.5
    kh = k.astype(f32).reshape(b, s, GLA_HEADS, GLA_DK)
    vh = v.astype(f32).reshape(b, s, GLA_HEADS, GLA_DV)
    lr2 = lr.astype(f32).reshape(b, s, 2, GLA_LR)
    logg = jax.nn.log_sigmoid(jnp.einsum('bsnr,nrk->bsnk', lr2, gk_w.astype(f32)) + gk_b.astype(f32)) / GLA_NORMALIZER
    logg = logg.reshape(b, s, 2, GLA_HEADS, GLA_DK)
    s0 = s0.astype(f32)
    o_f, st_f = gla_chunk_scan(qh, kh, vh, logg[:, :, 0], s0[:, 0])
    flip = lambda a: jnp.flip(a, axis=1)
    o_b, st_b = gla_chunk_scan(flip(qh), flip(kh), flip(vh), flip(logg[:, :, 1]), s0[:, 1])
    o = o_f + flip(o_b)
    o = rms_norm(o, onorm_g) * jax.nn.silu(gate.astype(f32).reshape(b, s, GLA_HEADS, GLA_DV))
    return o.reshape(b, s, GLA_HEADS * GLA_DV).astype(q.dtype), jnp.stack([st_f, st_b], axis=1)


def mla_mixer(c_q, c_kv, k_pe, qnorm_g, w_uq, kvnorm_g, w_ukv, q_g, k_g, pos, ctx):
    b, s, _ = c_q.shape
    q = rms_norm((rms_norm(c_q, qnorm_g) @ w_uq).reshape(b, s, MLA_HEADS, MLA_QK), q_g)
    c_kv = rms_norm(c_kv, kvnorm_g)

    def expand(ckv, kpe):
        kv = (ckv @ w_ukv).reshape(ckv.shape[0], ckv.shape[1], MLA_HEADS, MLA_NOPE + MLA_V)
        kr = jnp.broadcast_to(kpe[:, :, None, :], kpe.shape[:2] + (MLA_HEADS, MLA_ROPE))
        return rms_norm(jnp.concatenate([kv[..., :MLA_NOPE], kr], axis=-1), k_g), kv[..., MLA_NOPE:]

    k, v = expand(c_kv, k_pe)
    if pos is not None:
        row, col = pos
        q = jnp.concatenate([q[..., :MLA_NOPE], axial_rope(q[..., MLA_NOPE:], row, col)], axis=-1)
        k = jnp.concatenate([k[..., :MLA_NOPE], axial_rope(k[..., MLA_NOPE:], row, col)], axis=-1)
    if ctx is not None:
        kc, vc = expand(ctx[..., :MLA_KV_RANK], ctx[..., MLA_KV_RANK:])
        k = jnp.concatenate([kc, k], axis=1)
        v = jnp.concatenate([vc, v], axis=1)
    scale = MLA_QK ** -0.5

    def blk(qb):
        sc = jnp.einsum('bqhd,bkhd->bhqk', qb, k).astype(jnp.float32) * scale
        p = jax.nn.softmax(sc, axis=-1).astype(v.dtype)
        return jnp.einsum('bhqk,bkhd->bqhd', p, v)

    o = query_blocks(blk, q)
    return o.reshape(b, s, MLA_HEADS * MLA_V), jnp.concatenate([c_kv, k_pe], axis=-1)


def diff_mixer(q, k, v, q_g, k_g, lam_p, onorm_g, lam_init, pos, ctx):
    b, s, _ = q.shape
    q = rms_norm(q.reshape(b, s, DIFF_HEADS, 2, DIFF_DQK), q_g)
    k = rms_norm(k.reshape(b, s, DIFF_HEADS, 2, DIFF_DQK), k_g)
    v = v.reshape(b, s, DIFF_HEADS, DIFF_DV)
    k_out = k.reshape(b, s, DIFF_HEADS, 2 * DIFF_DQK)
    if pos is not None:
        row, col = pos
        q = axial_rope(q, row, col)
        k = axial_rope(k, row, col)
    if ctx is not None:
        kc, vc = ctx
        k = jnp.concatenate([kc.reshape(b, kc.shape[1], DIFF_HEADS, 2, DIFF_DQK), k], axis=1)
        v = jnp.concatenate([vc, v], axis=1)
    lp = lam_p.astype(jnp.float32)
    lam = jnp.exp(jnp.sum(lp[0] * lp[1])) - jnp.exp(jnp.sum(lp[2] * lp[3])) + lam_init
    scale = DIFF_DQK ** -0.5

    def blk(qb):
        sc = jnp.einsum('bqhnd,bkhnd->bnhqk', qb, k).astype(jnp.float32) * scale
        p = jax.nn.softmax(sc, axis=-1)
        a = (p[:, 0] - lam * p[:, 1]).astype(v.dtype)
        return jnp.einsum('bhqk,bkhd->bqhd', a, v)

    o = query_blocks(blk, q)
    o = rms_norm(o, onorm_g) * (1.0 - lam_init)
    return o.reshape(b, s, DIFF_HEADS * DIFF_DV), k_out, v


def pool_mixer(h, w, scale):
    b, s, ch = h.shape
    cs = jnp.cumsum(h.astype(jnp.float32), axis=1)
    cs = jnp.concatenate([jnp.zeros((b, 1, ch), jnp.float32), cs], axis=1)
    t = jnp.arange(s)
    outs = []
    for gi, win in enumerate(POOL_WINDOWS):
        lo = jnp.clip(t - win // 2, 0, s)
        hi = jnp.clip(t - win // 2 + win, 0, s)
        csg = cs[:, :, gi * POOL_CH:(gi + 1) * POOL_CH]
        mean = (csg[:, hi] - csg[:, lo]) / (hi - lo).astype(jnp.float32)[None, :, None]
        y = (mean - h[:, :, gi * POOL_CH:(gi + 1) * POOL_CH].astype(jnp.float32)).astype(h.dtype)
        outs.append(y @ w[gi])
    return jnp.concatenate(outs, axis=-1) * scale


def peer_ffn(h, wq, keys, u_tab, v_tab):
    b, s, d = h.shape
    t = b * s
    blk = min(PEER_BLOCK, t)

    def block_fn(xb):
        q = (xb @ wq).reshape(xb.shape[0], PEER_HEADS, 2, PEER_DHALF)
        sc = jnp.einsum('thpd,hpnd->thpn', q, keys).astype(jnp.float32)
        s_top, i_top = lax.top_k(sc, PEER_TOPK)
        cand = (s_top[:, :, 0, :, None] + s_top[:, :, 1, None, :]).reshape(-1, PEER_HEADS, PEER_TOPK * PEER_TOPK)
        cidx = (i_top[:, :, 0, :, None] * PEER_NKEYS + i_top[:, :, 1, None, :]).reshape(-1, PEER_HEADS, PEER_TOPK * PEER_TOPK)
        best, sel = lax.top_k(cand, PEER_TOPK)
        eidx = jnp.take_along_axis(cidx, sel, axis=-1)
        gate = jax.nn.softmax(best, axis=-1)
        act = jax.nn.gelu(jnp.einsum('thkd,td->thk', u_tab[eidx], xb).astype(jnp.float32))
        coef = (gate * act).astype(xb.dtype)
        return jnp.einsum('thk,thkd->td', coef, v_tab[eidx])

    out = lax.map(block_fn, h.reshape(t // blk, blk, d))
    return out.reshape(b, s, d)


def trunk_layer(x, cond, p, lidx, pos, ctx):
    b = x.shape[0]
    mod = (jax.nn.silu(cond) @ p['ada_w'] + p['ada_b'])[:, None, :]
    sh_a, sc_a, g_a, sh_f, sc_f, g_f = jnp.split(mod, 6, axis=-1)
    h = rms_norm(x, p['norm1_g']) * (1 + sc_a) + sh_a
    offsets = np.cumsum(IN_SPLITS)[:-1].tolist()
    gq, gk, gv, gg, glr, mcq, mckv, mkpe, dq, dk, dv, pin = jnp.split(h @ p['w_in'], offsets, axis=-1)
    if ctx is None:
        s0 = jnp.zeros((b, 2, GLA_HEADS, GLA_DK, GLA_DV), jnp.float32)
        mla_ctx = None
        diff_ctx = None
    else:
        s0, mla_ctx, dk_ctx, dv_ctx = ctx
        diff_ctx = (dk_ctx, dv_ctx)
    o_gla, st_gla = gla_mixer(gq, gk, gv, gg, glr, p['gla_gk_w'], p['gla_gk_b'], p['gla_onorm_g'], s0)
    o_mla, kv_mla = mla_mixer(mcq, mckv, mkpe, p['mla_qnorm_g'], p['mla_w_uq'], p['mla_kvnorm_g'],
                              p['mla_w_ukv'], p['mla_q_g'], p['mla_k_g'], pos, mla_ctx)
    lam_init = 0.8 - 0.6 * math.exp(-0.3 * lidx)
    o_diff, k_diff, v_diff = diff_mixer(dq, dk, dv, p['diff_q_g'], p['diff_k_g'], p['diff_lambda'],
                                        p['diff_onorm_g'], lam_init, pos, diff_ctx)
    o_pool = pool_mixer(pin, p['pool_w'], p['pool_scale'])
    mix = jnp.concatenate([o_gla, o_mla, o_diff, o_pool], axis=-1) @ p['w_out']
    x = x + g_a * mix
    h2 = rms_norm(x, p['norm2_g']) * (1 + sc_f) + sh_f
    x = x + g_f * peer_ffn(h2, p['peer_wq'], p['peer_keys'], p['peer_u'], p['peer_v'])
    return x, (st_gla, kv_mla, k_diff, v_diff)


def setup_inputs(seed: int = 0) -> dict:
    key = jax.random.key(seed)
    ks = iter(jax.random.split(key, 40))
    nrm = lambda shape, scale: jax.random.normal(next(ks), shape, jnp.float32) * scale
    gain = lambda shape: 1.0 + 0.02 * jax.random.normal(next(ks), shape, jnp.float32)
    L, D = DEPTH, D_MODEL
    return {
        'x_prompt': nrm((BATCH, SEQ, D), 1.0),
        'x_sample': nrm((DEC_BATCH, DEC_SEQ, D), 1.0),
        'state_gla': nrm((DEC_BATCH, L, 2, GLA_HEADS, GLA_DK, GLA_DV), 1.0),
        'cache_mla': nrm((DEC_BATCH, L, PAST_LEN, MLA_KV_RANK + MLA_ROPE), 1.0),
        'cache_diff_k': nrm((DEC_BATCH, L, PAST_LEN, DIFF_HEADS, 2 * DIFF_DQK), 1.0),
        'cache_diff_v': nrm((DEC_BATCH, L, PAST_LEN, DIFF_HEADS, DIFF_DV), 1.0),
        'c': nrm((DEC_BATCH, D), 1.0),
        'c_ctx': nrm((D,), 1.0),
        'ada_w': nrm((L, D, 6 * D), 0.5 * D ** -0.5),
        'ada_b': nrm((L, 6 * D), 0.02),
        'norm1_g': gain((L, D)),
        'norm2_g': gain((L, D)),
        'w_in': nrm((L, D, IN_COLS), D ** -0.5),
        'w_out': nrm((L, D, D), D ** -0.5),
        'gla_gk_w': nrm((L, 2, GLA_LR, GLA_HEADS * GLA_DK), GLA_LR ** -0.5),
        'gla_gk_b': nrm((L, 2, GLA_HEADS * GLA_DK), 0.1),
        'gla_onorm_g': gain((L, GLA_DV)),
        'mla_qnorm_g': gain((L, MLA_Q_RANK)),
        'mla_w_uq': nrm((L, MLA_Q_RANK, MLA_HEADS * MLA_QK), MLA_Q_RANK ** -0.5),
        'mla_kvnorm_g': gain((L, MLA_KV_RANK)),
        'mla_w_ukv': nrm((L, MLA_KV_RANK, MLA_HEADS * (MLA_NOPE + MLA_V)), MLA_KV_RANK ** -0.5),
        'mla_q_g': gain((L, MLA_QK)),
        'mla_k_g': gain((L, MLA_QK)),
        'diff_q_g': gain((L, DIFF_DQK)),
        'diff_k_g': gain((L, DIFF_DQK)),
        'diff_lambda': nrm((L, 4, DIFF_DQK), 0.1),
        'diff_onorm_g': gain((L, DIFF_DV)),
        'pool_w': nrm((L, len(POOL_WINDOWS), POOL_CH, POOL_CH), POOL_CH ** -0.5),
        'pool_scale': gain((L, GROUP_W)),
        'peer_wq': nrm((L, D, PEER_HEADS * PEER_DKEY), D ** -0.5),
        'peer_keys': nrm((L, PEER_HEADS, 2, PEER_NKEYS, PEER_DHALF), PEER_DHALF ** -0.5),
        'peer_u': nrm((L, PEER_EXPERTS, D), D ** -0.5),
        'peer_v': nrm((L, PEER_EXPERTS, D), 1.0),
    }


def reference(x_prompt, x_sample, state_gla, cache_mla, cache_diff_k, cache_diff_v, c, c_ctx,
              ada_w, ada_b, norm1_g, norm2_g, w_in, w_out, gla_gk_w, gla_gk_b, gla_onorm_g,
              mla_qnorm_g, mla_w_uq, mla_kvnorm_g, mla_w_ukv, mla_q_g, mla_k_g,
              diff_q_g, diff_k_g, diff_lambda, diff_onorm_g, pool_w, pool_scale,
              peer_wq, peer_keys, peer_u, peer_v):
    rows = x_sample.shape[1] // GRID_W
    pos = grid_positions(rows)
    y_prompt = x_prompt
    y_sample = x_sample
    st_gla, st_mla, st_dk, st_dv = [], [], [], []
    for l in range(DEPTH):
        p = {
            'ada_w': ada_w[l], 'ada_b': ada_b[l], 'norm1_g': norm1_g[l], 'norm2_g': norm2_g[l],
            'w_in': w_in[l], 'w_out': w_out[l],
            'gla_gk_w': gla_gk_w[l], 'gla_gk_b': gla_gk_b[l], 'gla_onorm_g': gla_onorm_g[l],
            'mla_qnorm_g': mla_qnorm_g[l], 'mla_w_uq': mla_w_uq[l], 'mla_kvnorm_g': mla_kvnorm_g[l],
            'mla_w_ukv': mla_w_ukv[l], 'mla_q_g': mla_q_g[l], 'mla_k_g': mla_k_g[l],
            'diff_q_g': diff_q_g[l], 'diff_k_g': diff_k_g[l], 'diff_lambda': diff_lambda[l],
            'diff_onorm_g': diff_onorm_g[l], 'pool_w': pool_w[l], 'pool_scale': pool_scale[l],
            'peer_wq': peer_wq[l], 'peer_keys': peer_keys[l], 'peer_u': peer_u[l], 'peer_v': peer_v[l],
        }
        y_prompt, (g_st, m_kv, d_k, d_v) = trunk_layer(y_prompt, c_ctx[None, :], p, l, None, None)
        st_gla.append(g_st)
        st_mla.append(m_kv)
        st_dk.append(d_k)
        st_dv.append(d_v)
        cached = (state_gla[:, l], cache_mla[:, l], cache_diff_k[:, l], cache_diff_v[:, l])
        y_sample, _ = trunk_layer(y_sample, c, p, l, pos, cached)
    new_state_gla = jnp.stack(st_gla, axis=1).astype(x_prompt.dtype)
    new_cache_mla = jnp.stack(st_mla, axis=1)
    new_cache_diff_k = jnp.stack(st_dk, axis=1)
    new_cache_diff_v = jnp.stack(st_dv, axis=1)
    return (y_prompt, y_sample, new_state_gla, new_cache_mla, new_cache_diff_k, new_cache_diff_v)
```

```python
import functools
import math

import jax
import jax.numpy as jnp
import numpy as np
from jax import lax
from jax.experimental import pallas as pl
from jax.experimental.pallas import tpu as pltpu

F32 = jnp.float32
BF16 = jnp.bfloat16
HI = lax.Precision.HIGHEST

D = 2048
DEPTH = 2
GRID_W = 64
GROUP_W = 512
GLA_H, GLA_DK, GLA_DV, GLA_LR, GLA_NORM, GLA_CHUNK = 4, 64, 128, 16, 16.0, 64
MLA_H, MLA_NOPE, MLA_ROPE, MLA_V, MLA_QK = 4, 128, 64, 128, 192
MLA_Q_RANK, MLA_KV_RANK = 384, 128
DIFF_H, DIFF_DV, DIFF_DQK = 4, 128, 64
POOL_WINDOWS = (2, 4, 8, 16)
POOL_CH = 128
PEER_H, PEER_NKEYS, PEER_TOPK, PEER_DHALF = 8, 128, 16, 128
PEER_EXPERTS = PEER_NKEYS * PEER_NKEYS
ROPE_BASE = 10000.0
EPS = 1e-6

LANES = 128
VMEM_LIMIT = 56 * 1024 * 1024

GLA_COLS = 1664
MLA_COLS = 640
DIFF_COLS = 1536
POOL_COLS = 512

ROW_BLOCK = 256
G_PITCH = 136


def _cparams(sem):
    return pltpu.CompilerParams(dimension_semantics=sem, vmem_limit_bytes=VMEM_LIMIT)


def _dot(a, b, precision=None):
    return jnp.dot(a, b, preferred_element_type=F32, precision=precision)


def _dot_nt(a, b, precision=None):
    return lax.dot_general(a, b, (((1,), (1,)), ((), ())), preferred_element_type=F32,
                           precision=precision)


def _dot_tn(a, b, precision=None):
    return lax.dot_general(a, b, (((0,), (0,)), ((), ())), preferred_element_type=F32,
                           precision=precision)


def _silu(x):
    return x / (1.0 + jnp.exp(-x))


def _ada_kernel(c_ref, w_ref, b_ref, o_ref):
    a = _silu(c_ref[...]).astype(BF16)
    o_ref[0] = _dot(a, w_ref[0].astype(BF16)) + b_ref[0]


def _ada_mod(cond16, ada_w, ada_b):
    L, _, N = ada_w.shape
    tn = 1024
    return pl.pallas_call(
        _ada_kernel,
        out_shape=jax.ShapeDtypeStruct((L, 16, N), F32),
        grid=(L, N // tn),
        in_specs=[pl.BlockSpec((16, D), lambda l, j: (0, 0)),
                  pl.BlockSpec((1, D, tn), lambda l, j: (l, 0, j)),
                  pl.BlockSpec((1, 1, tn), lambda l, j: (l, 0, j))],
        out_specs=pl.BlockSpec((1, 16, tn), lambda l, j: (l, 0, j)),
        compiler_params=_cparams(("arbitrary", "arbitrary")),
        name="ada_mod",
    )(cond16, ada_w, ada_b.reshape(L, 1, N))


def _mod_row_fn(n_ctx_blocks, blocks_per_dec):
    def row(i):
        return jnp.where(i < n_ctx_blocks, 0, 1 + (i - n_ctx_blocks) // blocks_per_dec)
    return row


def _mod_spec(row, chunk):
    return pl.BlockSpec((1, 1, D), lambda i: (row(i), 0, chunk))


def _normmod_kernel(x_ref, g_ref, sc_ref, sh_ref, o_ref):
    x = x_ref[...]
    y = x * lax.rsqrt(jnp.mean(x * x, axis=-1, keepdims=True) + EPS) * g_ref[...]
    o_ref[...] = (y * (1.0 + sc_ref[0]) + sh_ref[0]).astype(BF16)


def _normmod(x, g, mod3, row, sc_chunk, sh_chunk):
    T = x.shape[0]
    return pl.pallas_call(
        _normmod_kernel,
        out_shape=jax.ShapeDtypeStruct((T, D), BF16),
        grid=(T // ROW_BLOCK,),
        in_specs=[pl.BlockSpec((ROW_BLOCK, D), lambda i: (i, 0)),
                  pl.BlockSpec((1, D), lambda i: (0, 0)),
                  _mod_spec(row, sc_chunk), _mod_spec(row, sh_chunk)],
        out_specs=pl.BlockSpec((ROW_BLOCK, D), lambda i: (i, 0)),
        compiler_params=_cparams(("parallel",)),
        name="normmod",
    )(x, g.reshape(1, D), mod3, mod3)


def _mm_kernel(h_ref, w_ref, o_ref):
    o_ref[...] = _dot(h_ref[...], w_ref[...]).astype(o_ref.dtype)


def _matmul(h, w, tm=512, out_dtype=F32, name="matmul"):
    T, K = h.shape
    N = w.shape[1]
    return pl.pallas_call(
        _mm_kernel,
        out_shape=jax.ShapeDtypeStruct((T, N), out_dtype),
        grid=(T // tm,),
        in_specs=[pl.BlockSpec((tm, K), lambda i: (i, 0)),
                  pl.BlockSpec((K, N), lambda i: (0, 0))],
        out_specs=pl.BlockSpec((tm, N), lambda i: (i, 0)),
        compiler_params=_cparams(("parallel",)),
        name=name,
    )(h, w)


def _gla_kernel(*refs, S, has_state, emit_state, unroll):
    if has_state:
        p_ref, gkw_ref, gkb_ref, on_ref, s0_ref = refs[:5]
        rest = refs[5:]
    else:
        p_ref, gkw_ref, gkb_ref, on_ref = refs[:4]
        s0_ref = None
        rest = refs[4:]
    if emit_state:
        o_ref, st_ref, lg_scr, of_scr, st_scr = rest
    else:
        o_ref, lg_scr, of_scr, st_scr = rest
        st_ref = None
    C = GLA_CHUNK
    n = S // C
    H, DK, DV = GLA_H, GLA_DK, GLA_DV
    r_i = lax.broadcasted_iota(jnp.int32, (C, C), 0)
    c_i = lax.broadcasted_iota(jnp.int32, (C, C), 1)
    ones_cv = jnp.ones((C, DV), F32)
    on_g = on_ref[...]

    for d in (0, 1):
        lr = p_ref[:, 1536 + GLA_LR * d:1536 + GLA_LR * (d + 1)]
        x = _dot(lr, gkw_ref[d], HI) + gkb_ref[d]
        lg_scr[...] = (jnp.minimum(x, 0.0) - jnp.log(1.0 + jnp.exp(-jnp.abs(x)))) * (1.0 / GLA_NORM)
        if d == 0:
            cmat = (r_i >= c_i).astype(F32)
            causal = r_i >= c_i
        else:
            cmat = (r_i <= c_i).astype(F32)
            causal = r_i <= c_i
        for h in range(H):
            if has_state:
                st_scr[h] = s0_ref[0, d, h]
            else:
                st_scr[h] = jnp.zeros((DK, DV), F32)

        def chunk(ci, carry, d=d, cmat=cmat, causal=causal):
            c = ci if d == 0 else n - 1 - ci
            r0 = pl.multiple_of(c * C, C)
            gc = lg_scr[pl.ds(r0, C), :]
            cum = _dot(cmat, gc, HI)
            q = p_ref[pl.ds(r0, C), 0:256] * (DK ** -0.5)
            k = p_ref[pl.ds(r0, C), 256:512]
            qe = (q * jnp.exp(cum)).astype(BF16)
            ke = (k * jnp.exp(-cum)).astype(BF16)
            tot = jnp.sum(gc, axis=0, keepdims=True)
            kd = (k * jnp.exp(tot - cum)).astype(BF16)
            outs = []
            for h in range(H):
                sl = slice(h * DK, (h + 1) * DK)
                v_h = p_ref[pl.ds(r0, C), 512 + h * DV:512 + (h + 1) * DV].astype(BF16)
                st = st_scr[h]
                att = _dot_nt(qe[:, sl], ke[:, sl])
                att = jnp.where(causal, att, 0.0).astype(BF16)
                o_h = _dot(qe[:, sl], st.astype(BF16)) + _dot(att, v_h)
                dec = jnp.exp(_dot_tn(gc[:, sl], ones_cv, HI))
                st_scr[h] = dec * st + _dot_tn(kd[:, sl], v_h)
                outs.append(o_h)
            o_c = jnp.concatenate(outs, axis=1)
            if d == 0:
                of_scr[pl.ds(r0, C), :] = o_c
            else:
                o_c = o_c + of_scr[pl.ds(r0, C), :]
                gate = p_ref[pl.ds(r0, C), 1024:1536]
                res = []
                for h in range(H):
                    oh = o_c[:, h * DV:(h + 1) * DV]
                    oh = oh * lax.rsqrt(jnp.mean(oh * oh, axis=-1, keepdims=True) + EPS) * on_g
                    res.append(oh * _silu(gate[:, h * DV:(h + 1) * DV]))
                o_ref[pl.ds(r0, C), :] = jnp.concatenate(res, axis=1).astype(o_ref.dtype)
            return carry

        lax.fori_loop(0, n, chunk, 0, unroll=unroll)
        if emit_state:
            for h in range(H):
                st_ref[0, d, h] = st_scr[h]


def _gla(proj, gk_w, gk_b, onorm_g, s0, *, nb, S, blk0, emit_state):
    has_state = s0 is not None
    in_specs = [pl.BlockSpec((S, GLA_COLS), lambda b: (blk0 + b, 0)),
                pl.BlockSpec((2, GLA_LR, GLA_H * GLA_DK), lambda b: (0, 0, 0)),
                pl.BlockSpec((2, 1, GLA_H * GLA_DK), lambda b: (0, 0, 0)),
                pl.BlockSpec((1, GLA_DV), lambda b: (0, 0))]
    args = [proj, gk_w, gk_b.reshape(2, 1, -1), onorm_g.reshape(1, -1)]
    st_block = (1, 2, GLA_H, GLA_DK, GLA_DV)
    if has_state:
        in_specs.append(pl.BlockSpec(st_block, lambda b: (b, 0, 0, 0, 0)))
        args.append(s0)
    out_shape = [jax.ShapeDtypeStruct((nb * S, GLA_H * GLA_DV), BF16)]
    out_specs = [pl.BlockSpec((S, GLA_H * GLA_DV), lambda b: (b, 0))]
    if emit_state:
        out_shape.append(jax.ShapeDtypeStruct((nb,) + st_block[1:], F32))
        out_specs.append(pl.BlockSpec(st_block, lambda b: (b, 0, 0, 0, 0)))
    res = pl.pallas_call(
        functools.partial(_gla_kernel, S=S, has_state=has_state, emit_state=emit_state,
                          unroll=2),
        out_shape=out_shape, grid=(nb,), in_specs=in_specs, out_specs=out_specs,
        scratch_shapes=[pltpu.VMEM((S, GLA_H * GLA_DK), F32),
                        pltpu.VMEM((S, GLA_H * GLA_DV), F32),
                        pltpu.VMEM((GLA_H, GLA_DK, GLA_DV), F32)],
        compiler_params=_cparams(("parallel",)),
        name="gla_dec" if has_state else "gla_ctx",
    )(*args)
    return res if emit_state else (res[0], None)


def _rope_tables(n_tokens):
    t = jnp.arange(n_tokens)
    row = (t // GRID_W).astype(F32)
    col = (t % GRID_W).astype(F32)
    half = 16
    freq = ROPE_BASE ** (-jnp.arange(half, dtype=F32) / half)
    ar = row[:, None] * freq[None, :]
    ac = col[:, None] * freq[None, :]
    cos = jnp.concatenate([jnp.cos(ar), jnp.cos(ar), jnp.cos(ac), jnp.cos(ac)], axis=1)
    sin = jnp.concatenate([-jnp.sin(ar), jnp.sin(ar), -jnp.sin(ac), jnp.sin(ac)], axis=1)
    return jnp.tile(cos, (1, 2)), jnp.tile(sin, (1, 2))


def _rope128(x, cos, sin):
    lane = lax.broadcasted_iota(jnp.int32, x.shape, 1)
    first = (lane % 32) < 16
    partner = jnp.where(first, pltpu.roll(x, LANES - 16, 1), pltpu.roll(x, 16, 1))
    return x * cos + partner * sin


def _mla_kernel(*refs, S, P, rope, qb):
    if P:
        (p_ref, qn_ref, wuq_ref, kvn_ref, wukv_ref, qg_ref, kg_ref, cache_ref, cos_ref, sin_ref,
         o_ref, new_ref, k_scr, v_scr) = refs
    else:
        (p_ref, qn_ref, wuq_ref, kvn_ref, wukv_ref, qg_ref, kg_ref,
         o_ref, new_ref, k_scr, v_scr) = refs
    H = MLA_H
    NK = P + S
    qg = qg_ref[...]
    kg = kg_ref[...]
    qg_n, qg_r = qg[:, :MLA_NOPE], qg[:, MLA_NOPE:]
    kg_n, kg_r = kg[:, :MLA_NOPE], kg[:, MLA_NOPE:]
    qg_r2 = jnp.concatenate([qg_r, qg_r], axis=1)
    kg_r2 = jnp.concatenate([kg_r, kg_r], axis=1)

    ckv = p_ref[:, 384:512]
    ckv = ckv * lax.rsqrt(jnp.mean(ckv * ckv, axis=-1, keepdims=True) + EPS) * kvn_ref[...]
    kpe = p_ref[:, 512:576]
    new_ref[0, :, 0:MLA_KV_RANK] = ckv
    new_ref[0, :, MLA_KV_RANK:] = kpe

    def expand(c, pe, do_rope, r0, nrows):
        kv = _dot(c.astype(BF16), wukv_ref[...])
        pe2 = jnp.concatenate([pe, pe], axis=1)
        pe_ss = jnp.sum(pe * pe, axis=-1, keepdims=True)
        pe_g = pe2 * kg_r2
        if do_rope:
            pe_g = _rope128(pe_g, cos_ref[...], sin_ref[...])
        for h in range(H):
            kn = kv[:, h * MLA_NOPE:(h + 1) * MLA_NOPE]
            r = lax.rsqrt((jnp.sum(kn * kn, axis=-1, keepdims=True) + pe_ss) / MLA_QK + EPS)
            kfull = jnp.concatenate([kn * kg_n * r, pe_g[:, :MLA_ROPE] * r], axis=1)
            k_scr[h, r0:r0 + nrows, :] = kfull.astype(BF16)
            v_scr[h, r0:r0 + nrows, :] = kv[:, 512 + h * MLA_V:512 + (h + 1) * MLA_V].astype(BF16)

    if P:
        cc = cache_ref[0, 0]
        expand(cc[:, :MLA_KV_RANK], cc[:, MLA_KV_RANK:], False, 0, P)
    expand(ckv, kpe, rope, P, S)

    cq = p_ref[:, 0:384]
    cq = cq * lax.rsqrt(jnp.mean(cq * cq, axis=-1, keepdims=True) + EPS) * qn_ref[...]
    qall = _dot(cq.astype(BF16), wuq_ref[...])
    scale = MLA_QK ** -0.5
    lane = lax.broadcasted_iota(jnp.int32, (S, LANES), 1)
    for hp in range(H // 2):
        rs = []
        qns = []
        for j in range(2):
            h = 2 * hp + j
            qn = qall[:, h * MLA_NOPE:(h + 1) * MLA_NOPE]
            qr = qall[:, 512 + h * MLA_ROPE:512 + (h + 1) * MLA_ROPE]
            ss = jnp.sum(qn * qn, axis=-1, keepdims=True) + jnp.sum(qr * qr, axis=-1, keepdims=True)
            r = lax.rsqrt(ss / MLA_QK + EPS)
            rs.append(r)
            qns.append(qn * qg_n * r)
        qr2 = qall[:, 512 + hp * LANES:512 + (hp + 1) * LANES] * qg_r2
        qr2 = qr2 * jnp.where(lane < MLA_ROPE, rs[0], rs[1])
        if rope:
            qr2 = _rope128(qr2, cos_ref[...], sin_ref[...])
        for j in range(2):
            h = 2 * hp + j
            qfull = jnp.concatenate([qns[j], qr2[:, j * MLA_ROPE:(j + 1) * MLA_ROPE]], axis=1)
            qfull = (qfull * scale).astype(BF16)
            kh = k_scr[h]
            vh = v_scr[h]
            for i in range(S // qb):
                sc = _dot_nt(qfull[i * qb:(i + 1) * qb], kh)
                m = jnp.max(sc, axis=-1, keepdims=True)
                e = jnp.exp(sc - m)
                p = (e / jnp.sum(e, axis=-1, keepdims=True)).astype(BF16)
                o_ref[i * qb:(i + 1) * qb, h * MLA_V:(h + 1) * MLA_V] = _dot(p, vh).astype(o_ref.dtype)


def _mla(proj, qnorm_g, w_uq, kvnorm_g, w_ukv, q_g, k_g, cache, tables, *, nb, S, blk0, lidx):
    P = 0 if cache is None else cache.shape[2]
    const2 = lambda b: (0, 0)
    in_specs = [pl.BlockSpec((S, MLA_COLS), lambda b: (blk0 + b, 0)),
                pl.BlockSpec((1, MLA_Q_RANK), const2),
                pl.BlockSpec((MLA_Q_RANK, MLA_H * MLA_QK), const2),
                pl.BlockSpec((1, MLA_KV_RANK), const2),
                pl.BlockSpec((MLA_KV_RANK, MLA_H * (MLA_NOPE + MLA_V)), const2),
                pl.BlockSpec((1, MLA_QK), const2),
                pl.BlockSpec((1, MLA_QK), const2)]
    args = [proj, qnorm_g.reshape(1, -1), w_uq, kvnorm_g.reshape(1, -1), w_ukv,
            q_g.reshape(1, -1), k_g.reshape(1, -1)]
    if P:
        in_specs += [pl.BlockSpec((1, 1, P, MLA_KV_RANK + MLA_ROPE), lambda b: (b, lidx, 0, 0)),
                     pl.BlockSpec((S, LANES), const2), pl.BlockSpec((S, LANES), const2)]
        args += [cache, tables[0], tables[1]]
    return pl.pallas_call(
        functools.partial(_mla_kernel, S=S, P=P, rope=bool(P), qb=min(S, 256)),
        out_shape=[jax.ShapeDtypeStruct((nb * S, MLA_H * MLA_V), BF16),
                   jax.ShapeDtypeStruct((nb, S, MLA_KV_RANK + MLA_ROPE), F32)],
        grid=(nb,), in_specs=in_specs,
        out_specs=[pl.BlockSpec((S, MLA_H * MLA_V), lambda b: (b, 0)),
                   pl.BlockSpec((1, S, MLA_KV_RANK + MLA_ROPE), lambda b: (b, 0, 0))],
        scratch_shapes=[pltpu.VMEM((MLA_H, P + S, MLA_QK), BF16),
                        pltpu.VMEM((MLA_H, P + S, MLA_V), BF16)],
        compiler_params=_cparams(("parallel",)),
        name="mla_dec" if P else "mla_ctx",
    )(*args)


def _group_rms64(x, g2):
    lane = lax.broadcasted_iota(jnp.int32, x.shape, 1)
    lo = lane < DIFF_DQK
    xx = x * x
    s_all = jnp.sum(xx, axis=-1, keepdims=True)
    s_lo = jnp.sum(jnp.where(lo, xx, 0.0), axis=-1, keepdims=True)
    ms = jnp.where(lo, s_lo, s_all - s_lo) * (1.0 / DIFF_DQK)
    return x * lax.rsqrt(ms + EPS) * g2


def _diff_kernel(*refs, S, P, rope, qb, lam_init):
    if P:
        (p_ref, qg_ref, kg_ref, lam_ref, on_ref, kc_ref, vc_ref, cos_ref, sin_ref,
         o_ref, ko_ref, vo_ref, k_scr, v_scr) = refs
    else:
        (p_ref, qg_ref, kg_ref, lam_ref, on_ref, o_ref, ko_ref, vo_ref, k_scr, v_scr) = refs
    H = DIFF_H
    qg2 = jnp.concatenate([qg_ref[...], qg_ref[...]], axis=1)
    kg2 = jnp.concatenate([kg_ref[...], kg_ref[...]], axis=1)
    lp = lam_ref[...]
    l1 = jnp.sum(jnp.sum(lp[0:1] * lp[1:2], axis=-1, keepdims=True), axis=0, keepdims=True)
    l2 = jnp.sum(jnp.sum(lp[2:3] * lp[3:4], axis=-1, keepdims=True), axis=0, keepdims=True)
    lam = jnp.exp(l1) - jnp.exp(l2) + lam_init
    scale = DIFF_DQK ** -0.5
    lane = lax.broadcasted_iota(jnp.int32, (S, LANES), 1)
    lo = lane < DIFF_DQK
    on_g = on_ref[...]

    vo_ref[0] = p_ref[:, 1024:1536]
    for h in range(H):
        sl = slice(h * LANES, (h + 1) * LANES)
        kn = _group_rms64(p_ref[:, 512 + h * LANES:512 + (h + 1) * LANES], kg2)
        ko_ref[0, :, sl] = kn
        if rope:
            kn = _rope128(kn, cos_ref[...], sin_ref[...])
        if P:
            k_scr[0:P, :] = kc_ref[0, 0, :, sl].astype(BF16)
            v_scr[0:P, :] = vc_ref[0, 0, :, sl].astype(BF16)
        k_scr[P:P + S, :] = kn.astype(BF16)
        v_scr[P:P + S, :] = p_ref[:, 1024 + h * LANES:1024 + (h + 1) * LANES].astype(BF16)
        qn = _group_rms64(p_ref[:, sl], qg2)
        if rope:
            qn = _rope128(qn, cos_ref[...], sin_ref[...])
        qn = qn * scale
        q0 = jnp.where(lo, qn, 0.0).astype(BF16)
        q1 = jnp.where(lo, 0.0, qn).astype(BF16)
        kh = k_scr[...]
        vh = v_scr[...]
        for i in range(S // qb):
            rows = slice(i * qb, (i + 1) * qb)
            ps = []
            for qq in (q0, q1):
                sc = _dot_nt(qq[rows], kh)
                m = jnp.max(sc, axis=-1, keepdims=True)
                e = jnp.exp(sc - m)
                ps.append(e / jnp.sum(e, axis=-1, keepdims=True))
            a = (ps[0] - lam * ps[1]).astype(BF16)
            o = _dot(a, vh)
            o = o * lax.rsqrt(jnp.mean(o * o, axis=-1, keepdims=True) + EPS) * on_g
            o_ref[rows, sl] = (o * (1.0 - lam_init)).astype(o_ref.dtype)


def _diff(proj, q_g, k_g, lam_p, onorm_g, kc, vc, tables, *, nb, S, blk0, lidx):
    P = 0 if kc is None else kc.shape[2]
    const2 = lambda b: (0, 0)
    in_specs = [pl.BlockSpec((S, DIFF_COLS), lambda b: (blk0 + b, 0)),
                pl.BlockSpec((1, DIFF_DQK), const2), pl.BlockSpec((1, DIFF_DQK), const2),
                pl.BlockSpec((4, DIFF_DQK), const2), pl.BlockSpec((1, DIFF_DV), const2)]
    args = [proj, q_g.reshape(1, -1), k_g.reshape(1, -1), lam_p, onorm_g.reshape(1, -1)]
    if P:
        cspec = pl.BlockSpec((1, 1, P, DIFF_H * LANES), lambda b: (b, lidx, 0, 0))
        in_specs += [cspec, cspec, pl.BlockSpec((S, LANES), const2), pl.BlockSpec((S, LANES), const2)]
        args += [kc, vc, tables[0], tables[1]]
    lam_init = 0.8 - 0.6 * math.exp(-0.3 * lidx)
    ospec = pl.BlockSpec((1, S, DIFF_H * LANES), lambda b: (b, 0, 0))
    return pl.pallas_call(
        functools.partial(_diff_kernel, S=S, P=P, rope=bool(P), qb=min(S, 256), lam_init=lam_init),
        out_shape=[jax.ShapeDtypeStruct((nb * S, DIFF_H * DIFF_DV), BF16),
                   jax.ShapeDtypeStruct((nb, S, DIFF_H * LANES), F32),
                   jax.ShapeDtypeStruct((nb, S, DIFF_H * LANES), F32)],
        grid=(nb,), in_specs=in_specs,
        out_specs=[pl.BlockSpec((S, DIFF_H * DIFF_DV), lambda b: (b, 0)), ospec, ospec],
        scratch_shapes=[pltpu.VMEM((P + S, LANES), BF16), pltpu.VMEM((P + S, LANES), BF16)],
        compiler_params=_cparams(("parallel",)),
        name="diff_dec" if P else "diff_ctx",
    )(*args)


POOL_PAD = 8


def _pool_kernel(p_ref, w_ref, sc_ref, o_ref, pad_scr, *, S):
    zeros = jnp.zeros((POOL_PAD, GROUP_W), F32)
    pad_scr[0:POOL_PAD, :] = zeros
    pad_scr[POOL_PAD + S:POOL_PAD + S + POOL_PAD, :] = zeros
    pad_scr[POOL_PAD:POOL_PAD + S, :] = p_ref[...]
    t = lax.broadcasted_iota(jnp.int32, (S, POOL_CH), 0)
    for gi, win in enumerate(POOL_WINDOWS):
        cs = slice(gi * POOL_CH, (gi + 1) * POOL_CH)
        acc = None
        for dlt in range(-(win // 2), win // 2):
            piece = pad_scr[POOL_PAD + dlt:POOL_PAD + dlt + S, cs]
            acc = piece if acc is None else acc + piece
        cnt = (jnp.minimum(t + win // 2, S) - jnp.maximum(t - win // 2, 0)).astype(F32)
        y = (acc / cnt - p_ref[:, cs]).astype(BF16)
        o_ref[:, cs] = (_dot(y, w_ref[gi]) * sc_ref[:, cs]).astype(o_ref.dtype)


def _pool(proj, w, scale, *, nb, S, blk0):
    return pl.pallas_call(
        functools.partial(_pool_kernel, S=S),
        out_shape=jax.ShapeDtypeStruct((nb * S, GROUP_W), BF16),
        grid=(nb,),
        in_specs=[pl.BlockSpec((S, POOL_COLS), lambda b: (blk0 + b, 0)),
                  pl.BlockSpec((len(POOL_WINDOWS), POOL_CH, POOL_CH), lambda b: (0, 0, 0)),
                  pl.BlockSpec((1, GROUP_W), lambda b: (0, 0))],
        out_specs=pl.BlockSpec((S, GROUP_W), lambda b: (b, 0)),
        scratch_shapes=[pltpu.VMEM((S + 2 * POOL_PAD, GROUP_W), F32)],
        compiler_params=_cparams(("parallel",)),
        name="pool",
    )(proj, w, scale.reshape(1, -1))


def _outproj_kernel(og_ref, om_ref, od_ref, op_ref, w_ref, x_ref, ga_ref, g2_ref, sc_ref, sh_ref,
                    x1_ref, h2_ref):
    mix = _dot(og_ref[...], w_ref[0:512, :])
    mix += _dot(om_ref[...], w_ref[512:1024, :])
    mix += _dot(od_ref[...], w_ref[1024:1536, :])
    mix += _dot(op_ref[...], w_ref[1536:2048, :])
    x1 = x_ref[...] + ga_ref[0] * mix
    x1_ref[...] = x1
    y = x1 * lax.rsqrt(jnp.mean(x1 * x1, axis=-1, keepdims=True) + EPS) * g2_ref[...]
    h2_ref[...] = (y * (1.0 + sc_ref[0]) + sh_ref[0]).astype(BF16)


def _outproj(og, om, od, op, w_out, x, mod3, norm2_g, row):
    T = x.shape[0]
    ospec = pl.BlockSpec((ROW_BLOCK, GROUP_W), lambda i: (i, 0))
    xspec = pl.BlockSpec((ROW_BLOCK, D), lambda i: (i, 0))
    return pl.pallas_call(
        _outproj_kernel,
        out_shape=[jax.ShapeDtypeStruct((T, D), F32), jax.ShapeDtypeStruct((T, D), BF16)],
        grid=(T // ROW_BLOCK,),
        in_specs=[ospec, ospec, ospec, ospec,
                  pl.BlockSpec((D, D), lambda i: (0, 0)),
                  xspec, _mod_spec(row, 2),
                  pl.BlockSpec((1, D), lambda i: (0, 0)),
                  _mod_spec(row, 4), _mod_spec(row, 3)],
        out_specs=[xspec, xspec],
        compiler_params=_cparams(("parallel",)),
        name="outproj",
    )(og, om, od, op, w_out, x, mod3, norm2_g.reshape(1, D), mod3, mod3)


def _topk_rows(x_ref, nrows, tt):
    rows = lax.broadcasted_iota(jnp.int32, (nrows, tt), 0).astype(F32)
    krow = lax.broadcasted_iota(jnp.int32, (PEER_TOPK, tt), 0)

    def body(k, carry):
        vals, ids = carry
        x = x_ref[...]
        m = jnp.max(x, axis=0, keepdims=True)
        idx = jnp.min(jnp.where(x == m, rows, float(nrows)), axis=0, keepdims=True)
        x_ref[...] = jnp.where(rows == idx, -jnp.inf, x)
        vals = jnp.where(krow == k, m, vals)
        ids = jnp.where(krow == k, idx, ids)
        return vals, ids

    init = (jnp.zeros((PEER_TOPK, tt), F32), jnp.zeros((PEER_TOPK, tt), F32))
    return lax.fori_loop(0, PEER_TOPK, body, init)


def _peer_topk_kernel(h_ref, wq_ref, keys_ref, i_ref, j_ref, g_ref,
                      q_scr, sc_scr, cand_scr, i_scr, j_scr, g_scr, *, tt):
    q = _dot(h_ref[...], wq_ref[...]).astype(BF16)
    for hp in range(2 * PEER_H):
        q_scr[hp] = q[:, hp * PEER_DHALF:(hp + 1) * PEER_DHALF]
    K = PEER_TOPK

    def head(h, carry):
        tops = []
        for p in range(2):
            sc_scr[...] = _dot_nt(keys_ref[h, p], q_scr[2 * h + p])
            tops.append(_topk_rows(sc_scr, PEER_NKEYS, tt))
        (s0, i0), (s1, i1) = tops
        for a in range(K):
            cand_scr[a * K:(a + 1) * K, :] = s0[a:a + 1, :] + s1
        best, flat = _topk_rows(cand_scr, K * K, tt)
        flat = flat.astype(jnp.int32)
        a_sel = flat >> 4
        b_sel = flat & (K - 1)
        i_sel = jnp.zeros((K, tt), F32)
        j_sel = jnp.zeros((K, tt), F32)
        for a in range(K):
            i_sel = jnp.where(a_sel == a, i0[a:a + 1, :], i_sel)
            j_sel = jnp.where(b_sel == a, i1[a:a + 1, :], j_sel)
        e = jnp.exp(best - best[0:1, :])
        i_scr[h] = i_sel
        j_scr[h] = j_sel
        g_scr[h] = e / jnp.sum(e, axis=0, keepdims=True)
        return carry

    lax.fori_loop(0, PEER_H, head, 0)
    nrow = PEER_H * K
    i_ref[...] = i_scr[...].reshape(nrow, tt).T.astype(jnp.int32)
    j_ref[...] = j_scr[...].reshape(nrow, tt).T.astype(jnp.int32)
    g_ref[...] = g_scr[...].reshape(nrow, tt).T


def _peer_topk(h2, wq, keys, tt=ROW_BLOCK):
    T = h2.shape[0]
    npick = PEER_H * PEER_TOPK
    ospec = pl.BlockSpec((tt, npick), lambda i: (i, 0))
    return pl.pallas_call(
        functools.partial(_peer_topk_kernel, tt=tt),
        out_shape=[jax.ShapeDtypeStruct((T, npick), jnp.int32),
                   jax.ShapeDtypeStruct((T, npick), jnp.int32),
                   jax.ShapeDtypeStruct((T, npick), F32)],
        grid=(T // tt,),
        in_specs=[pl.BlockSpec((tt, D), lambda i: (i, 0)),
                  pl.BlockSpec((D, D), lambda i: (0, 0)),
                  pl.BlockSpec((PEER_H, 2, PEER_NKEYS, PEER_DHALF), lambda i: (0, 0, 0, 0))],
        out_specs=[ospec, ospec, ospec],
        scratch_shapes=[pltpu.VMEM((2 * PEER_H, tt, PEER_DHALF), BF16),
                        pltpu.VMEM((PEER_NKEYS, tt), F32),
                        pltpu.VMEM((PEER_TOPK * PEER_TOPK, tt), F32),
                        pltpu.VMEM((PEER_H, PEER_TOPK, tt), F32),
                        pltpu.VMEM((PEER_H, PEER_TOPK, tt), F32),
                        pltpu.VMEM((PEER_H, PEER_TOPK, tt), F32)],
        compiler_params=_cparams(("parallel",)),
        name="peer_topk",
    )(h2, wq, keys)


def _gelu_tanh(x):
    return 0.5 * x * (1.0 + jnp.tanh(0.7978845608028654 * (x + 0.044715 * x * x * x)))


def _peer_expert_kernel(h_ref, i_ref, j_ref, g_ref, ut_ref, v_ref, x_ref, gf_ref, o_ref,
                        gate_scr, acc_scr, *, tb, et):
    e = pl.program_id(1)
    ne = pl.num_programs(1)
    nk = PEER_NKEYS

    @pl.when(e == 0)
    def _():
        acc_scr[...] = jnp.zeros_like(acc_scr)
        sub = lax.broadcasted_iota(jnp.int32, (nk, PEER_H * PEER_TOPK), 0)

        def tok(t, carry):
            irow = i_ref[pl.ds(t, 1), :]
            jrow = j_ref[pl.ds(t, 1), :]
            grow = g_ref[pl.ds(t, 1), :]
            yt = jnp.where(sub == irow, 1.0, 0.0).astype(BF16)
            xt = jnp.where(sub == jrow, grow, 0.0).astype(BF16)
            gate_scr[pl.ds(pl.multiple_of(t * G_PITCH, 8), nk), :] = _dot_nt(yt, xt)
            return carry

        lax.fori_loop(0, tb, tok, 0)

    s = _dot(h_ref[...], ut_ref[...])
    i0 = e * (et // nk)
    gate = jnp.concatenate(
        [gate_scr[pl.ds(i0 + il, tb, stride=G_PITCH), :] for il in range(et // nk)], axis=1)
    c = (gate * _gelu_tanh(s)).astype(BF16)
    acc_scr[...] += _dot(c, v_ref[...])

    @pl.when(e == ne - 1)
    def _():
        o_ref[...] = x_ref[...] + gf_ref[0] * acc_scr[...]


def _peer_expert(h2, ii, jj, gg, ut, v, x1, mod3, row, tb=ROW_BLOCK, et=512):
    T = h2.shape[0]
    ne = PEER_EXPERTS // et
    npick = PEER_H * PEER_TOPK
    tspec = pl.BlockSpec((tb, npick), lambda i, e: (i, 0))
    xspec = pl.BlockSpec((tb, D), lambda i, e: (i, 0))
    return pl.pallas_call(
        functools.partial(_peer_expert_kernel, tb=tb, et=et),
        out_shape=jax.ShapeDtypeStruct((T, D), F32),
        grid=(T // tb, ne),
        in_specs=[xspec, tspec, tspec, tspec,
                  pl.BlockSpec((D, et), lambda i, e: (0, e)),
                  pl.BlockSpec((et, D), lambda i, e: (e, 0)),
                  xspec,
                  pl.BlockSpec((1, 1, D), lambda i, e: (row(i), 0, 5))],
        out_specs=xspec,
        scratch_shapes=[pltpu.VMEM((tb * G_PITCH, LANES), F32),
                        pltpu.VMEM((tb, D), F32)],
        compiler_params=_cparams(("parallel", "arbitrary")),
        name="peer_expert",
    )(h2, ii, jj, gg, ut, v, x1, mod3)


def _pack_w_in(w):
    o = np.cumsum([0, 256, 256, 512, 512, 32, 384, 128, 64, 512, 512, 512, 512]).tolist()
    z = lambda n: jnp.zeros((D, n), w.dtype)
    w_gla = jnp.concatenate([w[:, o[0]:o[5]], z(GLA_COLS - 1568)], axis=1)
    w_mla = jnp.concatenate([w[:, o[5]:o[8]], z(MLA_COLS - 576)], axis=1)
    w_diff = w[:, o[8]:o[11]]
    w_pool = w[:, o[11]:o[12]]
    return [a.astype(BF16) for a in (w_gla, w_mla, w_diff, w_pool)]


def _pack_w_uq(w):
    w = w.reshape(MLA_Q_RANK, MLA_H, MLA_QK)
    return jnp.concatenate([w[:, :, :MLA_NOPE].reshape(MLA_Q_RANK, -1),
                            w[:, :, MLA_NOPE:].reshape(MLA_Q_RANK, -1)], axis=1).astype(BF16)


def _pack_w_ukv(w):
    w = w.reshape(MLA_KV_RANK, MLA_H, MLA_NOPE + MLA_V)
    return jnp.concatenate([w[:, :, :MLA_NOPE].reshape(MLA_KV_RANK, -1),
                            w[:, :, MLA_NOPE:].reshape(MLA_KV_RANK, -1)], axis=1).astype(BF16)


def kernel(x_prompt, x_sample, state_gla, cache_mla, cache_diff_k, cache_diff_v, c, c_ctx,
           ada_w, ada_b, norm1_g, norm2_g, w_in, w_out, gla_gk_w, gla_gk_b, gla_onorm_g,
           mla_qnorm_g, mla_w_uq, mla_kvnorm_g, mla_w_ukv, mla_q_g, mla_k_g,
           diff_q_g, diff_k_g, diff_lambda, diff_onorm_g, pool_w, pool_scale,
           peer_wq, peer_keys, peer_u, peer_v):
    B, S, _ = x_prompt.shape
    DB, DS, _ = x_sample.shape
    L = ada_w.shape[0]
    t_ctx = B * S
    t_dec = DB * DS
    assert t_ctx % ROW_BLOCK == 0 and DS % ROW_BLOCK == 0 and t_ctx % DS == 0 and DB + 1 <= 16
    row = _mod_row_fn(t_ctx // ROW_BLOCK, DS // ROW_BLOCK)

    x = jnp.concatenate([x_prompt.reshape(t_ctx, D), x_sample.reshape(t_dec, D)], axis=0)
    cond = jnp.concatenate([c_ctx[None, :], c, jnp.zeros((15 - DB, D), F32)], axis=0)
    mod = _ada_mod(cond, ada_w, ada_b)
    tables = _rope_tables(DS)
    kc_all = cache_diff_k.reshape(DB, L, -1, DIFF_H * LANES)
    vc_all = cache_diff_v.reshape(DB, L, -1, DIFF_H * LANES)

    st_gla, st_mla, st_dk, st_dv = [], [], [], []
    for l in range(L):
        mod3 = mod[l].reshape(16, 1, 6 * D)
        w_gla, w_mla, w_diff, w_pool = _pack_w_in(w_in[l])
        h = _normmod(x, norm1_g[l], mod3, row, 1, 0)
        p_gla = _matmul(h, w_gla, name="in_gla")
        p_mla = _matmul(h, w_mla, name="in_mla")
        p_diff = _matmul(h, w_diff, name="in_diff")
        p_pool = _matmul(h, w_pool, name="in_pool")

        w_uq = _pack_w_uq(mla_w_uq[l])
        w_ukv = _pack_w_ukv(mla_w_ukv[l])
        dec_blk0 = t_ctx // DS
        og_c, st = _gla(p_gla, gla_gk_w[l], gla_gk_b[l], gla_onorm_g[l], None,
                        nb=B, S=S, blk0=0, emit_state=True)
        om_c, kv_c = _mla(p_mla, mla_qnorm_g[l], w_uq, mla_kvnorm_g[l], w_ukv, mla_q_g[l], mla_k_g[l],
                          None, None, nb=B, S=S, blk0=0, lidx=l)
        od_c, dk_c, dv_c = _diff(p_diff, diff_q_g[l], diff_k_g[l], diff_lambda[l], diff_onorm_g[l],
                                 None, None, None, nb=B, S=S, blk0=0, lidx=l)
        op_c = _pool(p_pool, pool_w[l].astype(BF16), pool_scale[l], nb=B, S=S, blk0=0)
        og_d, _ = _gla(p_gla, gla_gk_w[l], gla_gk_b[l], gla_onorm_g[l], state_gla[:, l],
                       nb=DB, S=DS, blk0=dec_blk0, emit_state=False)
        om_d, _ = _mla(p_mla, mla_qnorm_g[l], w_uq, mla_kvnorm_g[l], w_ukv, mla_q_g[l], mla_k_g[l],
                       cache_mla, tables, nb=DB, S=DS, blk0=dec_blk0, lidx=l)
        od_d, _, _ = _diff(p_diff, diff_q_g[l], diff_k_g[l], diff_lambda[l], diff_onorm_g[l],
                           kc_all, vc_all, tables, nb=DB, S=DS, blk0=dec_blk0, lidx=l)
        op_d = _pool(p_pool, pool_w[l].astype(BF16), pool_scale[l], nb=DB, S=DS, blk0=dec_blk0)
        st_gla.append(st)
        st_mla.append(kv_c)
        st_dk.append(dk_c.reshape(B, S, DIFF_H, 2 * DIFF_DQK))
        st_dv.append(dv_c.reshape(B, S, DIFF_H, DIFF_DV))

        cat = lambda a, b: jnp.concatenate([a, b], axis=0)
        x1, h2 = _outproj(cat(og_c, og_d), cat(om_c, om_d), cat(od_c, od_d), cat(op_c, op_d),
                          w_out[l].astype(BF16), x, mod3, norm2_g[l], row)
        ii, jj, gg = _peer_topk(h2, peer_wq[l].astype(BF16), peer_keys[l].astype(BF16))
        x = _peer_expert(h2, ii, jj, gg, peer_u[l].T.astype(BF16), peer_v[l].astype(BF16),
                         x1, mod3, row)

    y_prompt = x[:t_ctx].reshape(B, S, D)
    y_sample = x[t_ctx:].reshape(DB, DS, D)
    return (y_prompt, y_sample, jnp.stack(st_gla, axis=1), jnp.stack(st_mla, axis=1),
            jnp.stack(st_dk, axis=1), jnp.stack(st_dv, axis=1))
```

```python
import functools
import math

import jax
import jax.numpy as jnp
import numpy as np
from jax import lax
from jax.experimental import pallas as pl
from jax.experimental.pallas import tpu as pltpu

F32 = jnp.float32
BF16 = jnp.bfloat16
HI = lax.Precision.HIGHEST

D = 2048
DEPTH = 2
GRID_W = 64
GROUP_W = 512
GLA_H, GLA_DK, GLA_DV, GLA_LR, GLA_NORM, GLA_CHUNK = 4, 64, 128, 16, 16.0, 64
MLA_H, MLA_NOPE, MLA_ROPE, MLA_V, MLA_QK = 4, 128, 64, 128, 192
MLA_Q_RANK, MLA_KV_RANK = 384, 128
DIFF_H, DIFF_DV, DIFF_DQK = 4, 128, 64
POOL_WINDOWS = (2, 4, 8, 16)
POOL_CH = 128
PEER_H, PEER_NKEYS, PEER_TOPK, PEER_DHALF = 8, 128, 16, 128
PEER_EXPERTS = PEER_NKEYS * PEER_NKEYS
ROPE_BASE = 10000.0
EPS = 1e-6

LANES = 128
VMEM_LIMIT = 56 * 1024 * 1024

GLA_COLS = 1664
MLA_COLS = 640
DIFF_COLS = 1536
POOL_COLS = 512

ROW_BLOCK = 256
PEER_TB = 512
PEER_ET = 1024
G_HALF = 64
G_PITCH = 72


def _cparams(sem):
    return pltpu.CompilerParams(dimension_semantics=sem, vmem_limit_bytes=VMEM_LIMIT)


def _dot(a, b, precision=None):
    return jnp.dot(a, b, preferred_element_type=F32, precision=precision)


def _dot_nt(a, b, precision=None):
    return lax.dot_general(a, b, (((1,), (1,)), ((), ())), preferred_element_type=F32,
                           precision=precision)


def _dot_tn(a, b, precision=None):
    return lax.dot_general(a, b, (((0,), (0,)), ((), ())), preferred_element_type=F32,
                           precision=precision)


def _silu(x):
    return x / (1.0 + jnp.exp(-x))


def _ada_kernel(c_ref, w_ref, b_ref, o_ref):
    a = _silu(c_ref[...]).astype(BF16)
    o_ref[0] = _dot(a, w_ref[0].astype(BF16)) + b_ref[0]


def _ada_mod(cond16, ada_w, ada_b):
    L, _, N = ada_w.shape
    tn = 1024
    return pl.pallas_call(
        _ada_kernel,
        out_shape=jax.ShapeDtypeStruct((L, 16, N), F32),
        grid=(L, N // tn),
        in_specs=[pl.BlockSpec((16, D), lambda l, j: (0, 0)),
                  pl.BlockSpec((1, D, tn), lambda l, j: (l, 0, j)),
                  pl.BlockSpec((1, 1, tn), lambda l, j: (l, 0, j))],
        out_specs=pl.BlockSpec((1, 16, tn), lambda l, j: (l, 0, j)),
        compiler_params=_cparams(("arbitrary", "arbitrary")),
        name="ada_mod",
    )(cond16, ada_w, ada_b.reshape(L, 1, N))


def _mod_row_fn(n_ctx_blocks, blocks_per_dec):
    def row(i):
        return jnp.where(i < n_ctx_blocks, 0, 1 + (i - n_ctx_blocks) // blocks_per_dec)
    return row


def _mod_spec(row, chunk):
    return pl.BlockSpec((1, 1, D), lambda i: (row(i), 0, chunk))


def _normmod_kernel(x_ref, g_ref, sc_ref, sh_ref, o_ref):
    x = x_ref[...]
    y = x * lax.rsqrt(jnp.mean(x * x, axis=-1, keepdims=True) + EPS) * g_ref[...]
    o_ref[...] = (y * (1.0 + sc_ref[0]) + sh_ref[0]).astype(BF16)


def _normmod(x, g, mod3, row, sc_chunk, sh_chunk):
    T = x.shape[0]
    return pl.pallas_call(
        _normmod_kernel,
        out_shape=jax.ShapeDtypeStruct((T, D), BF16),
        grid=(T // ROW_BLOCK,),
        in_specs=[pl.BlockSpec((ROW_BLOCK, D), lambda i: (i, 0)),
                  pl.BlockSpec((1, D), lambda i: (0, 0)),
                  _mod_spec(row, sc_chunk), _mod_spec(row, sh_chunk)],
        out_specs=pl.BlockSpec((ROW_BLOCK, D), lambda i: (i, 0)),
        compiler_params=_cparams(("parallel",)),
        name="normmod",
    )(x, g.reshape(1, D), mod3, mod3)


def _mm_kernel(h_ref, w_ref, o_ref):
    o_ref[...] = _dot(h_ref[...], w_ref[...]).astype(o_ref.dtype)


def _matmul(h, w, tm=512, out_dtype=F32, name="matmul"):
    T, K = h.shape
    N = w.shape[1]
    return pl.pallas_call(
        _mm_kernel,
        out_shape=jax.ShapeDtypeStruct((T, N), out_dtype),
        grid=(T // tm,),
        in_specs=[pl.BlockSpec((tm, K), lambda i: (i, 0)),
                  pl.BlockSpec((K, N), lambda i: (0, 0))],
        out_specs=pl.BlockSpec((tm, N), lambda i: (i, 0)),
        compiler_params=_cparams(("parallel",)),
        name=name,
    )(h, w)


def _gla_kernel(*refs, S, has_state, emit_state, unroll):
    if has_state:
        p_ref, gkw_ref, gkb_ref, on_ref, s0_ref = refs[:5]
        rest = refs[5:]
    else:
        p_ref, gkw_ref, gkb_ref, on_ref = refs[:4]
        s0_ref = None
        rest = refs[4:]
    if emit_state:
        o_ref, st_ref, lg_scr, of_scr, st_scr = rest
    else:
        o_ref, lg_scr, of_scr, st_scr = rest
        st_ref = None
    C = GLA_CHUNK
    n = S // C
    H, DK, DV = GLA_H, GLA_DK, GLA_DV
    r_i = lax.broadcasted_iota(jnp.int32, (C, C), 0)
    c_i = lax.broadcasted_iota(jnp.int32, (C, C), 1)
    ones_cv = jnp.ones((C, DV), F32)
    on_g = on_ref[...]

    for d in (0, 1):
        lr = p_ref[:, 1536 + GLA_LR * d:1536 + GLA_LR * (d + 1)]
        x = _dot(lr, gkw_ref[d], HI) + gkb_ref[d]
        lg_scr[...] = (jnp.minimum(x, 0.0) - jnp.log(1.0 + jnp.exp(-jnp.abs(x)))) * (1.0 / GLA_NORM)
        if d == 0:
            cmat = (r_i >= c_i).astype(F32)
            causal = r_i >= c_i
        else:
            cmat = (r_i <= c_i).astype(F32)
            causal = r_i <= c_i
        for h in range(H):
            if has_state:
                st_scr[h] = s0_ref[0, 0, d, h]
            else:
                st_scr[h] = jnp.zeros((DK, DV), F32)

        def chunk(ci, carry, d=d, cmat=cmat, causal=causal):
            c = ci if d == 0 else n - 1 - ci
            r0 = pl.multiple_of(c * C, C)
            gc = lg_scr[pl.ds(r0, C), :]
            cum = _dot(cmat, gc, HI)
            q = p_ref[pl.ds(r0, C), 0:256] * (DK ** -0.5)
            k = p_ref[pl.ds(r0, C), 256:512]
            qe = (q * jnp.exp(cum)).astype(BF16)
            ke = (k * jnp.exp(-cum)).astype(BF16)
            tot = jnp.sum(gc, axis=0, keepdims=True)
            kd = (k * jnp.exp(tot - cum)).astype(BF16)
            outs = []
            for h in range(H):
                sl = slice(h * DK, (h + 1) * DK)
                v_h = p_ref[pl.ds(r0, C), 512 + h * DV:512 + (h + 1) * DV].astype(BF16)
                st = st_scr[h]
                att = _dot_nt(qe[:, sl], ke[:, sl])
                att = jnp.where(causal, att, 0.0).astype(BF16)
                o_h = _dot(qe[:, sl], st.astype(BF16)) + _dot(att, v_h)
                dec = jnp.exp(_dot_tn(gc[:, sl], ones_cv, HI))
                st_scr[h] = dec * st + _dot_tn(kd[:, sl], v_h)
                outs.append(o_h)
            o_c = jnp.concatenate(outs, axis=1)
            if d == 0:
                of_scr[pl.ds(r0, C), :] = o_c
            else:
                o_c = o_c + of_scr[pl.ds(r0, C), :]
                gate = p_ref[pl.ds(r0, C), 1024:1536]
                res = []
                for h in range(H):
                    oh = o_c[:, h * DV:(h + 1) * DV]
                    oh = oh * lax.rsqrt(jnp.mean(oh * oh, axis=-1, keepdims=True) + EPS) * on_g
                    res.append(oh * _silu(gate[:, h * DV:(h + 1) * DV]))
                o_ref[pl.ds(r0, C), :] = jnp.concatenate(res, axis=1).astype(o_ref.dtype)
            return carry

        lax.fori_loop(0, n, chunk, 0, unroll=unroll)
        if emit_state:
            for h in range(H):
                st_ref[0, 0, d, h] = st_scr[h]


def _carried_call(kern, *, grid, in_specs, args, outs, scratch_shapes, name):
    n_in = len(args)
    carried = [(k, o[2]) for k, o in enumerate(outs) if o[2] is not None]
    n_carried = len(carried)

    def body(*refs):
        kern(*refs[:n_in], *refs[n_in + n_carried:])

    return pl.pallas_call(
        body,
        out_shape=[o[0] for o in outs], grid=grid,
        in_specs=list(in_specs) + [pl.BlockSpec(memory_space=pl.ANY)] * n_carried,
        out_specs=[o[1] for o in outs],
        scratch_shapes=scratch_shapes,
        input_output_aliases={n_in + j: k for j, (k, _) in enumerate(carried)},
        compiler_params=_cparams(("parallel",)),
        name=name,
    )(*args, *[prev for _, prev in carried])


def _gla(proj, gk_w, gk_b, onorm_g, s0, *, nb, S, blk0, o_prev, st_prev, lidx, n_layers):
    has_state = s0 is not None
    emit_state = not has_state
    in_specs = [pl.BlockSpec((S, GLA_COLS), lambda b: (blk0 + b, 0)),
                pl.BlockSpec((2, GLA_LR, GLA_H * GLA_DK), lambda b: (0, 0, 0)),
                pl.BlockSpec((2, 1, GLA_H * GLA_DK), lambda b: (0, 0, 0)),
                pl.BlockSpec((1, GLA_DV), lambda b: (0, 0))]
    args = [proj, gk_w, gk_b.reshape(2, 1, -1), onorm_g.reshape(1, -1)]
    st_dims = (2, GLA_H, GLA_DK, GLA_DV)
    if has_state:
        in_specs.append(pl.BlockSpec((1, 1) + st_dims, lambda b: (b, lidx, 0, 0, 0, 0)))
        args.append(s0)
    outs = [(jax.ShapeDtypeStruct((proj.shape[0], GLA_H * GLA_DV), BF16),
             pl.BlockSpec((S, GLA_H * GLA_DV), lambda b: (blk0 + b, 0)), o_prev)]
    if emit_state:
        outs.append((jax.ShapeDtypeStruct((nb, n_layers) + st_dims, F32),
                     pl.BlockSpec((1, 1) + st_dims, lambda b: (b, lidx, 0, 0, 0, 0)), st_prev))
    res = _carried_call(
        functools.partial(_gla_kernel, S=S, has_state=has_state, emit_state=emit_state, unroll=2),
        grid=(nb,), in_specs=in_specs, args=args, outs=outs,
        scratch_shapes=[pltpu.VMEM((S, GLA_H * GLA_DK), F32),
                        pltpu.VMEM((S, GLA_H * GLA_DV), F32),
                        pltpu.VMEM((GLA_H, GLA_DK, GLA_DV), F32)],
        name="gla_dec" if has_state else "gla_ctx")
    return res if emit_state else (res[0], None)


def _rope_tables(n_tokens):
    t = jnp.arange(n_tokens)
    row = (t // GRID_W).astype(F32)
    col = (t % GRID_W).astype(F32)
    half = 16
    freq = ROPE_BASE ** (-jnp.arange(half, dtype=F32) / half)
    ar = row[:, None] * freq[None, :]
    ac = col[:, None] * freq[None, :]
    cos = jnp.concatenate([jnp.cos(ar), jnp.cos(ar), jnp.cos(ac), jnp.cos(ac)], axis=1)
    sin = jnp.concatenate([-jnp.sin(ar), jnp.sin(ar), -jnp.sin(ac), jnp.sin(ac)], axis=1)
    return jnp.tile(cos, (1, 2)), jnp.tile(sin, (1, 2))


def _rope128(x, cos, sin):
    lane = lax.broadcasted_iota(jnp.int32, x.shape, 1)
    first = (lane % 32) < 16
    partner = jnp.where(first, pltpu.roll(x, LANES - 16, 1), pltpu.roll(x, 16, 1))
    return x * cos + partner * sin


def _mla_kernel(*refs, S, P, rope, qb):
    if P:
        (p_ref, qn_ref, wuq_ref, kvn_ref, wukv_ref, qg_ref, kg_ref, cache_ref, cos_ref, sin_ref,
         o_ref, k_scr, v_scr) = refs
        new_ref = None
    else:
        (p_ref, qn_ref, wuq_ref, kvn_ref, wukv_ref, qg_ref, kg_ref,
         o_ref, new_ref, k_scr, v_scr) = refs
    H = MLA_H
    NK = P + S
    qg = qg_ref[...]
    kg = kg_ref[...]
    qg_n, qg_r = qg[:, :MLA_NOPE], qg[:, MLA_NOPE:]
    kg_n, kg_r = kg[:, :MLA_NOPE], kg[:, MLA_NOPE:]
    qg_r2 = jnp.concatenate([qg_r, qg_r], axis=1)
    kg_r2 = jnp.concatenate([kg_r, kg_r], axis=1)

    ckv = p_ref[:, 384:512]
    ckv = ckv * lax.rsqrt(jnp.mean(ckv * ckv, axis=-1, keepdims=True) + EPS) * kvn_ref[...]
    kpe = p_ref[:, 512:576]
    if new_ref is not None:
        new_ref[0, 0, :, 0:MLA_KV_RANK] = ckv
        new_ref[0, 0, :, MLA_KV_RANK:] = kpe

    def expand(c, pe, do_rope, r0, nrows):
        kv = _dot(c.astype(BF16), wukv_ref[...])
        pe2 = jnp.concatenate([pe, pe], axis=1)
        pe_ss = jnp.sum(pe * pe, axis=-1, keepdims=True)
        pe_g = pe2 * kg_r2
        if do_rope:
            pe_g = _rope128(pe_g, cos_ref[...], sin_ref[...])
        for h in range(H):
            kn = kv[:, h * MLA_NOPE:(h + 1) * MLA_NOPE]
            r = lax.rsqrt((jnp.sum(kn * kn, axis=-1, keepdims=True) + pe_ss) / MLA_QK + EPS)
            kfull = jnp.concatenate([kn * kg_n * r, pe_g[:, :MLA_ROPE] * r], axis=1)
            k_scr[h, r0:r0 + nrows, :] = kfull.astype(BF16)
            v_scr[h, r0:r0 + nrows, :] = kv[:, 512 + h * MLA_V:512 + (h + 1) * MLA_V].astype(BF16)

    if P:
        cc = cache_ref[0, 0]
        expand(cc[:, :MLA_KV_RANK], cc[:, MLA_KV_RANK:], False, 0, P)
    expand(ckv, kpe, rope, P, S)

    cq = p_ref[:, 0:384]
    cq = cq * lax.rsqrt(jnp.mean(cq * cq, axis=-1, keepdims=True) + EPS) * qn_ref[...]
    qall = _dot(cq.astype(BF16), wuq_ref[...])
    scale = MLA_QK ** -0.5
    lane = lax.broadcasted_iota(jnp.int32, (S, LANES), 1)
    for hp in range(H // 2):
        rs = []
        qns = []
        for j in range(2):
            h = 2 * hp + j
            qn = qall[:, h * MLA_NOPE:(h + 1) * MLA_NOPE]
            qr = qall[:, 512 + h * MLA_ROPE:512 + (h + 1) * MLA_ROPE]
            ss = jnp.sum(qn * qn, axis=-1, keepdims=True) + jnp.sum(qr * qr, axis=-1, keepdims=True)
            r = lax.rsqrt(ss / MLA_QK + EPS)
            rs.append(r)
            qns.append(qn * qg_n * r)
        qr2 = qall[:, 512 + hp * LANES:512 + (hp + 1) * LANES] * qg_r2
        qr2 = qr2 * jnp.where(lane < MLA_ROPE, rs[0], rs[1])
        if rope:
            qr2 = _rope128(qr2, cos_ref[...], sin_ref[...])
        for j in range(2):
            h = 2 * hp + j
            qfull = jnp.concatenate([qns[j], qr2[:, j * MLA_ROPE:(j + 1) * MLA_ROPE]], axis=1)
            qfull = (qfull * scale).astype(BF16)
            kh = k_scr[h]
            vh = v_scr[h]
            for i in range(S // qb):
                sc = _dot_nt(qfull[i * qb:(i + 1) * qb], kh)
                m = jnp.max(sc, axis=-1, keepdims=True)
                e = jnp.exp(sc - m)
                p = (e / jnp.sum(e, axis=-1, keepdims=True)).astype(BF16)
                o_ref[i * qb:(i + 1) * qb, h * MLA_V:(h + 1) * MLA_V] = _dot(p, vh).astype(o_ref.dtype)


def _mla(proj, qnorm_g, w_uq, kvnorm_g, w_ukv, q_g, k_g, cache, tables, *, nb, S, blk0, lidx,
         o_prev, new_prev, n_layers):
    P = 0 if cache is None else cache.shape[2]
    const2 = lambda b: (0, 0)
    in_specs = [pl.BlockSpec((S, MLA_COLS), lambda b: (blk0 + b, 0)),
                pl.BlockSpec((1, MLA_Q_RANK), const2),
                pl.BlockSpec((MLA_Q_RANK, MLA_H * MLA_QK), const2),
                pl.BlockSpec((1, MLA_KV_RANK), const2),
                pl.BlockSpec((MLA_KV_RANK, MLA_H * (MLA_NOPE + MLA_V)), const2),
                pl.BlockSpec((1, MLA_QK), const2),
                pl.BlockSpec((1, MLA_QK), const2)]
    args = [proj, qnorm_g.reshape(1, -1), w_uq, kvnorm_g.reshape(1, -1), w_ukv,
            q_g.reshape(1, -1), k_g.reshape(1, -1)]
    if P:
        in_specs += [pl.BlockSpec((1, 1, P, MLA_KV_RANK + MLA_ROPE), lambda b: (b, lidx, 0, 0)),
                     pl.BlockSpec((S, LANES), const2), pl.BlockSpec((S, LANES), const2)]
        args += [cache, tables[0], tables[1]]
    outs = [(jax.ShapeDtypeStruct((proj.shape[0], MLA_H * MLA_V), BF16),
             pl.BlockSpec((S, MLA_H * MLA_V), lambda b: (blk0 + b, 0)), o_prev)]
    if not P:
        cw = MLA_KV_RANK + MLA_ROPE
        outs.append((jax.ShapeDtypeStruct((nb, n_layers, S, cw), F32),
                     pl.BlockSpec((1, 1, S, cw), lambda b: (b, lidx, 0, 0)), new_prev))
    res = _carried_call(
        functools.partial(_mla_kernel, S=S, P=P, rope=bool(P), qb=min(S, 256)),
        grid=(nb,), in_specs=in_specs, args=args, outs=outs,
        scratch_shapes=[pltpu.VMEM((MLA_H, P + S, MLA_QK), BF16),
                        pltpu.VMEM((MLA_H, P + S, MLA_V), BF16)],
        name="mla_dec" if P else "mla_ctx")
    return (res[0], None) if P else res


def _group_rms64(x, g2):
    lane = lax.broadcasted_iota(jnp.int32, x.shape, 1)
    lo = lane < DIFF_DQK
    xx = x * x
    s_all = jnp.sum(xx, axis=-1, keepdims=True)
    s_lo = jnp.sum(jnp.where(lo, xx, 0.0), axis=-1, keepdims=True)
    ms = jnp.where(lo, s_lo, s_all - s_lo) * (1.0 / DIFF_DQK)
    return x * lax.rsqrt(ms + EPS) * g2


def _diff_kernel(*refs, S, P, rope, qb, lam_init):
    if P:
        (p_ref, qg_ref, kg_ref, lam_ref, on_ref, kc_ref, vc_ref, cos_ref, sin_ref,
         o_ref, k_scr, v_scr) = refs
        ko_ref = vo_ref = None
    else:
        (p_ref, qg_ref, kg_ref, lam_ref, on_ref, o_ref, ko_ref, vo_ref, k_scr, v_scr) = refs
    H = DIFF_H
    qg2 = jnp.concatenate([qg_ref[...], qg_ref[...]], axis=1)
    kg2 = jnp.concatenate([kg_ref[...], kg_ref[...]], axis=1)
    lp = lam_ref[...]
    l1 = jnp.sum(jnp.sum(lp[0:1] * lp[1:2], axis=-1, keepdims=True), axis=0, keepdims=True)
    l2 = jnp.sum(jnp.sum(lp[2:3] * lp[3:4], axis=-1, keepdims=True), axis=0, keepdims=True)
    lam = jnp.exp(l1) - jnp.exp(l2) + lam_init
    scale = DIFF_DQK ** -0.5
    lane = lax.broadcasted_iota(jnp.int32, (S, LANES), 1)
    lo = lane < DIFF_DQK
    on_g = on_ref[...]

    if vo_ref is not None:
        vo_ref[0, 0] = p_ref[:, 1024:1536]
    for h in range(H):
        sl = slice(h * LANES, (h + 1) * LANES)
        kn = _group_rms64(p_ref[:, 512 + h * LANES:512 + (h + 1) * LANES], kg2)
        if ko_ref is not None:
            ko_ref[0, 0, :, sl] = kn
        if rope:
            kn = _rope128(kn, cos_ref[...], sin_ref[...])
        if P:
            k_scr[0:P, :] = kc_ref[0, 0, :, sl].astype(BF16)
            v_scr[0:P, :] = vc_ref[0, 0, :, sl].astype(BF16)
        k_scr[P:P + S, :] = kn.astype(BF16)
        v_scr[P:P + S, :] = p_ref[:, 1024 + h * LANES:1024 + (h + 1) * LANES].astype(BF16)
        qn = _group_rms64(p_ref[:, sl], qg2)
        if rope:
            qn = _rope128(qn, cos_ref[...], sin_ref[...])
        qn = qn * scale
        q0 = jnp.where(lo, qn, 0.0).astype(BF16)
        q1 = jnp.where(lo, 0.0, qn).astype(BF16)
        kh = k_scr[...]
        vh = v_scr[...]
        for i in range(S // qb):
            rows = slice(i * qb, (i + 1) * qb)
            ps = []
            for qq in (q0, q1):
                sc = _dot_nt(qq[rows], kh)
                m = jnp.max(sc, axis=-1, keepdims=True)
                e = jnp.exp(sc - m)
                ps.append(e / jnp.sum(e, axis=-1, keepdims=True))
            a = (ps[0] - lam * ps[1]).astype(BF16)
            o = _dot(a, vh)
            o = o * lax.rsqrt(jnp.mean(o * o, axis=-1, keepdims=True) + EPS) * on_g
            o_ref[rows, sl] = (o * (1.0 - lam_init)).astype(o_ref.dtype)


def _diff(proj, q_g, k_g, lam_p, onorm_g, kc, vc, tables, *, nb, S, blk0, lidx,
          o_prev, k_prev, v_prev, n_layers):
    P = 0 if kc is None else kc.shape[2]
    const2 = lambda b: (0, 0)
    in_specs = [pl.BlockSpec((S, DIFF_COLS), lambda b: (blk0 + b, 0)),
                pl.BlockSpec((1, DIFF_DQK), const2), pl.BlockSpec((1, DIFF_DQK), const2),
                pl.BlockSpec((4, DIFF_DQK), const2), pl.BlockSpec((1, DIFF_DV), const2)]
    args = [proj, q_g.reshape(1, -1), k_g.reshape(1, -1), lam_p, onorm_g.reshape(1, -1)]
    if P:
        cspec = pl.BlockSpec((1, 1, P, DIFF_H * LANES), lambda b: (b, lidx, 0, 0))
        in_specs += [cspec, cspec, pl.BlockSpec((S, LANES), const2), pl.BlockSpec((S, LANES), const2)]
        args += [kc, vc, tables[0], tables[1]]
    lam_init = 0.8 - 0.6 * math.exp(-0.3 * lidx)
    outs = [(jax.ShapeDtypeStruct((proj.shape[0], DIFF_H * DIFF_DV), BF16),
             pl.BlockSpec((S, DIFF_H * DIFF_DV), lambda b: (blk0 + b, 0)), o_prev)]
    if not P:
        cshape = jax.ShapeDtypeStruct((nb, n_layers, S, DIFF_H * LANES), F32)
        cspec = pl.BlockSpec((1, 1, S, DIFF_H * LANES), lambda b: (b, lidx, 0, 0))
        outs += [(cshape, cspec, k_prev), (cshape, cspec, v_prev)]
    res = _carried_call(
        functools.partial(_diff_kernel, S=S, P=P, rope=bool(P), qb=min(S, 256), lam_init=lam_init),
        grid=(nb,), in_specs=in_specs, args=args, outs=outs,
        scratch_shapes=[pltpu.VMEM((P + S, LANES), BF16), pltpu.VMEM((P + S, LANES), BF16)],
        name="diff_dec" if P else "diff_ctx")
    return (res[0], None, None) if P else res


POOL_PAD = 8


def _pool_kernel(p_ref, w_ref, sc_ref, o_ref, pad_scr, *, S):
    zeros = jnp.zeros((POOL_PAD, GROUP_W), F32)
    pad_scr[0:POOL_PAD, :] = zeros
    pad_scr[POOL_PAD + S:POOL_PAD + S + POOL_PAD, :] = zeros
    pad_scr[POOL_PAD:POOL_PAD + S, :] = p_ref[...]
    t = lax.broadcasted_iota(jnp.int32, (S, POOL_CH), 0)
    for gi, win in enumerate(POOL_WINDOWS):
        cs = slice(gi * POOL_CH, (gi + 1) * POOL_CH)
        acc = None
        for dlt in range(-(win // 2), win // 2):
            piece = pad_scr[POOL_PAD + dlt:POOL_PAD + dlt + S, cs]
            acc = piece if acc is None else acc + piece
        cnt = (jnp.minimum(t + win // 2, S) - jnp.maximum(t - win // 2, 0)).astype(F32)
        y = (acc / cnt - p_ref[:, cs]).astype(BF16)
        o_ref[:, cs] = (_dot(y, w_ref[gi]) * sc_ref[:, cs]).astype(o_ref.dtype)


def _pool(proj, w, scale, *, nb, S, blk0, o_prev):
    return _carried_call(
        functools.partial(_pool_kernel, S=S),
        grid=(nb,),
        in_specs=[pl.BlockSpec((S, POOL_COLS), lambda b: (blk0 + b, 0)),
                  pl.BlockSpec((len(POOL_WINDOWS), POOL_CH, POOL_CH), lambda b: (0, 0, 0)),
                  pl.BlockSpec((1, GROUP_W), lambda b: (0, 0))],
        args=[proj, w, scale.reshape(1, -1)],
        outs=[(jax.ShapeDtypeStruct((proj.shape[0], GROUP_W), BF16),
               pl.BlockSpec((S, GROUP_W), lambda b: (blk0 + b, 0)), o_prev)],
        scratch_shapes=[pltpu.VMEM((S + 2 * POOL_PAD, GROUP_W), F32)],
        name="pool")[0]


def _outproj_kernel(og_ref, om_ref, od_ref, op_ref, w_ref, x_ref, ga_ref, g2_ref, sc_ref, sh_ref,
                    x1_ref, h2_ref):
    mix = _dot(og_ref[...], w_ref[0:512, :])
    mix += _dot(om_ref[...], w_ref[512:1024, :])
    mix += _dot(od_ref[...], w_ref[1024:1536, :])
    mix += _dot(op_ref[...], w_ref[1536:2048, :])
    x1 = x_ref[...] + ga_ref[0] * mix
    x1_ref[...] = x1
    y = x1 * lax.rsqrt(jnp.mean(x1 * x1, axis=-1, keepdims=True) + EPS) * g2_ref[...]
    h2_ref[...] = (y * (1.0 + sc_ref[0]) + sh_ref[0]).astype(BF16)


def _outproj(og, om, od, op, w_out, x, mod3, norm2_g, row):
    T = x.shape[0]
    ospec = pl.BlockSpec((ROW_BLOCK, GROUP_W), lambda i: (i, 0))
    xspec = pl.BlockSpec((ROW_BLOCK, D), lambda i: (i, 0))
    return pl.pallas_call(
        _outproj_kernel,
        out_shape=[jax.ShapeDtypeStruct((T, D), F32), jax.ShapeDtypeStruct((T, D), BF16)],
        grid=(T // ROW_BLOCK,),
        in_specs=[ospec, ospec, ospec, ospec,
                  pl.BlockSpec((D, D), lambda i: (0, 0)),
                  xspec, _mod_spec(row, 2),
                  pl.BlockSpec((1, D), lambda i: (0, 0)),
                  _mod_spec(row, 4), _mod_spec(row, 3)],
        out_specs=[xspec, xspec],
        compiler_params=_cparams(("parallel",)),
        name="outproj",
    )(og, om, od, op, w_out, x, mod3, norm2_g.reshape(1, D), mod3, mod3)


def _topk_chains(x_refs, ids, sentinel, tt):
    krow = lax.broadcasted_iota(jnp.int32, (PEER_TOPK, tt), 0)

    def body(k, carry):
        out = []
        for x_ref, (vals, sel) in zip(x_refs, carry):
            x = x_ref[...]
            m = jnp.max(x, axis=0, keepdims=True)
            idx = jnp.min(jnp.where(x == m, ids, sentinel), axis=0, keepdims=True)
            x_ref[...] = jnp.where(ids == idx, -jnp.inf, x)
            out.append((jnp.where(krow == k, m, vals), jnp.where(krow == k, idx, sel)))
        return tuple(out)

    zeros = jnp.zeros((PEER_TOPK, tt), F32)
    return lax.fori_loop(0, PEER_TOPK, body, tuple((zeros, zeros) for _ in x_refs))


CAND_ROWS = 80


def _cand_ids(tt):
    r = lax.broadcasted_iota(jnp.int32, (CAND_ROWS, tt), 0)
    mid = (1 + ((r - 16) >> 3)) * PEER_TOPK + ((r - 16) & 7)
    ids = jnp.where(r < 16, r, jnp.where(r < 72, mid, (r - 64) * PEER_TOPK))
    return ids.astype(F32)


def _peer_topk_kernel(h_ref, wq_ref, keys_ref, i_ref, j_ref, g_ref,
                      q_scr, sc_scr, cand_scr, i_scr, j_scr, g_scr, *, tt):
    q = _dot(h_ref[...], wq_ref[...]).astype(BF16)
    for hp in range(2 * PEER_H):
        q_scr[hp] = q[:, hp * PEER_DHALF:(hp + 1) * PEER_DHALF]
    K = PEER_TOPK
    key_ids = lax.broadcasted_iota(jnp.int32, (PEER_NKEYS, tt), 0).astype(F32)
    cand_ids = _cand_ids(tt)

    def head_pair(hh, carry):
        for u in range(2):
            for p in range(2):
                h = 2 * hh + u
                sc_scr[2 * u + p] = _dot_nt(keys_ref[h, p], q_scr[2 * h + p])
        tops = _topk_chains([sc_scr.at[c] for c in range(4)], key_ids, float(PEER_NKEYS), tt)
        for u in range(2):
            (s0, _), (s1, _) = tops[2 * u], tops[2 * u + 1]
            cand_scr[u, 0:K, :] = s0[0:1, :] + s1
            for a in range(1, 8):
                cand_scr[u, 8 + 8 * a:16 + 8 * a, :] = s0[a:a + 1, :] + s1[0:8, :]
            cand_scr[u, 72:80, :] = s0[8:16, :] + s1[0:1, :]
        picks = _topk_chains([cand_scr.at[u] for u in range(2)], cand_ids, float(K * K), tt)
        for u in range(2):
            (_, i0), (_, i1) = tops[2 * u], tops[2 * u + 1]
            best, flat = picks[u]
            flat = flat.astype(jnp.int32)
            a_sel = flat >> 4
            b_sel = flat & (K - 1)
            i_sel = jnp.zeros((K, tt), F32)
            j_sel = jnp.zeros((K, tt), F32)
            for a in range(K):
                i_sel = jnp.where(a_sel == a, i0[a:a + 1, :], i_sel)
                j_sel = jnp.where(b_sel == a, i1[a:a + 1, :], j_sel)
            e = jnp.exp(best - best[0:1, :])
            i_scr[2 * hh + u] = i_sel
            j_scr[2 * hh + u] = j_sel
            g_scr[2 * hh + u] = e / jnp.sum(e, axis=0, keepdims=True)
        return carry

    lax.fori_loop(0, PEER_H // 2, head_pair, 0)
    nrow = PEER_H * K
    i_ref[...] = i_scr[...].reshape(nrow, tt).T.astype(jnp.int32)
    j_ref[...] = j_scr[...].reshape(nrow, tt).T.astype(jnp.int32)
    g_ref[...] = g_scr[...].reshape(nrow, tt).T


def _peer_topk(h2, wq, keys, tt=ROW_BLOCK):
    T = h2.shape[0]
    npick = PEER_H * PEER_TOPK
    ospec = pl.BlockSpec((tt, npick), lambda i: (i, 0))
    return pl.pallas_call(
        functools.partial(_peer_topk_kernel, tt=tt),
        out_shape=[jax.ShapeDtypeStruct((T, npick), jnp.int32),
                   jax.ShapeDtypeStruct((T, npick), jnp.int32),
                   jax.ShapeDtypeStruct((T, npick), F32)],
        grid=(T // tt,),
        in_specs=[pl.BlockSpec((tt, D), lambda i: (i, 0)),
                  pl.BlockSpec((D, D), lambda i: (0, 0)),
                  pl.BlockSpec((PEER_H, 2, PEER_NKEYS, PEER_DHALF), lambda i: (0, 0, 0, 0))],
        out_specs=[ospec, ospec, ospec],
        scratch_shapes=[pltpu.VMEM((2 * PEER_H, tt, PEER_DHALF), BF16),
                        pltpu.VMEM((4, PEER_NKEYS, tt), F32),
                        pltpu.VMEM((2, CAND_ROWS, tt), F32),
                        pltpu.VMEM((PEER_H, PEER_TOPK, tt), F32),
                        pltpu.VMEM((PEER_H, PEER_TOPK, tt), F32),
                        pltpu.VMEM((PEER_H, PEER_TOPK, tt), F32)],
        compiler_params=_cparams(("parallel",)),
        name="peer_topk",
    )(h2, wq, keys)


def _gelu_tanh(x):
    return 0.5 * x * (1.0 + jnp.tanh(0.7978845608028654 * (x + 0.044715 * x * x * x)))


def _peer_expert_kernel(h_ref, i_ref, j_ref, g_ref, ut_ref, v_ref, o_ref, gate_scr, *, tb, et):
    e = pl.program_id(1)
    nk = PEER_NKEYS
    npick = PEER_H * PEER_TOPK
    tiles = et // nk

    @pl.when(e == 0)
    def _():
        o_ref[...] = jnp.zeros_like(o_ref)
        sub = lax.broadcasted_iota(jnp.int32, (nk, npick), 0)
        zero = jnp.zeros((nk, npick), BF16)

        def pair(tp, carry):
            lhs, rhs = [], []
            for u in range(2):
                t = 2 * tp + u
                irow = i_ref[pl.ds(t, 1), :]
                jrow = j_ref[pl.ds(t, 1), :]
                grow = g_ref[pl.ds(t, 1), :]
                lhs.append(jnp.where(sub == irow, 1.0, 0.0).astype(BF16))
                xt = jnp.where(sub == jrow, grow, 0.0).astype(BF16)
                rhs.append(jnp.concatenate([xt, zero] if u == 0 else [zero, xt], axis=1))
            g2 = _dot_nt(jnp.concatenate(lhs, axis=1), jnp.concatenate(rhs, axis=0))
            for u in range(2):
                g = g2[:, u * nk:(u + 1) * nk].astype(BF16).astype(F32)
                hi = pltpu.bitcast(g[:G_HALF], jnp.uint32)
                lo = pltpu.bitcast(g[G_HALF:], jnp.uint32)
                r0 = pl.multiple_of((2 * tp + u) * G_PITCH, 8)
                gate_scr[pl.ds(r0, G_HALF), :] = hi | (lo >> 16)
            return carry

        lax.fori_loop(0, tb // 2, pair, 0, unroll=4)

    s = _dot(h_ref[...], ut_ref[0])
    i0 = e * tiles
    upper = i0 < G_HALF
    r0 = jnp.where(upper, i0, i0 - G_HALF)
    words = jnp.concatenate(
        [gate_scr[pl.ds(r0 + il, tb, stride=G_PITCH), :] for il in range(tiles)], axis=1)
    bits = jnp.where(upper, words & jnp.uint32(0xFFFF0000), words << 16)
    c = (pltpu.bitcast(bits, F32) * _gelu_tanh(s)).astype(BF16)
    o_ref[...] += _dot(c, v_ref[...])


def _peer_expert(h2, ii, jj, gg, ut, v, tb=PEER_TB, et=PEER_ET):
    T = h2.shape[0]
    ne = PEER_EXPERTS // et
    assert G_HALF % (et // PEER_NKEYS) == 0
    npick = PEER_H * PEER_TOPK
    tspec = pl.BlockSpec((tb, npick), lambda i, e: (i, 0))
    xspec = pl.BlockSpec((tb, D), lambda i, e: (i, 0))
    return pl.pallas_call(
        functools.partial(_peer_expert_kernel, tb=tb, et=et),
        out_shape=jax.ShapeDtypeStruct((T, D), F32),
        grid=(T // tb, ne),
        in_specs=[xspec, tspec, tspec, tspec,
                  pl.BlockSpec((1, D, et), lambda i, e: (e, 0, 0)),
                  pl.BlockSpec((et, D), lambda i, e: (e, 0))],
        out_specs=xspec,
        scratch_shapes=[pltpu.VMEM((tb * G_PITCH, LANES), jnp.uint32)],
        compiler_params=_cparams(("parallel", "arbitrary")),
        name="peer_expert",
    )(h2, ii, jj, gg, ut, v)


def _resid_kernel(*refs, emit_h):
    if emit_h:
        x1_ref, p_ref, gf_ref, g_ref, sc_ref, sh_ref, x_ref, h_ref = refs
    else:
        x1_ref, p_ref, gf_ref, x_ref = refs
    x = x1_ref[...] + gf_ref[0] * p_ref[...]
    x_ref[...] = x
    if emit_h:
        y = x * lax.rsqrt(jnp.mean(x * x, axis=-1, keepdims=True) + EPS) * g_ref[...]
        h_ref[...] = (y * (1.0 + sc_ref[0]) + sh_ref[0]).astype(BF16)


def _resid(x1, peer, mod3, row, next_norm=None, blk0=0, nblk=None):
    nblk = x1.shape[0] // ROW_BLOCK if nblk is None else nblk
    ispec = pl.BlockSpec((ROW_BLOCK, D), lambda i: (blk0 + i, 0))
    ospec = pl.BlockSpec((ROW_BLOCK, D), lambda i: (i, 0))
    mspec = lambda chunk: pl.BlockSpec((1, 1, D), lambda i: (row(blk0 + i), 0, chunk))
    in_specs = [ispec, ispec, mspec(5)]
    args = [x1, peer, mod3]
    out_shape = [jax.ShapeDtypeStruct((nblk * ROW_BLOCK, D), F32)]
    out_specs = [ospec]
    if next_norm is not None:
        g, mod3n = next_norm
        in_specs += [pl.BlockSpec((1, D), lambda i: (0, 0)), mspec(1), mspec(0)]
        args += [g.reshape(1, D), mod3n, mod3n]
        out_shape.append(jax.ShapeDtypeStruct((nblk * ROW_BLOCK, D), BF16))
        out_specs.append(ospec)
    res = pl.pallas_call(
        functools.partial(_resid_kernel, emit_h=next_norm is not None),
        out_shape=out_shape, grid=(nblk,), in_specs=in_specs, out_specs=out_specs,
        compiler_params=_cparams(("parallel",)),
        name="resid",
    )(*args)
    return res if next_norm is not None else (res[0], None)


def _pack_w_in(w):
    o = np.cumsum([0, 256, 256, 512, 512, 32, 384, 128, 64, 512, 512, 512, 512]).tolist()
    z = lambda n: jnp.zeros((D, n), w.dtype)
    w_gla = jnp.concatenate([w[:, o[0]:o[5]], z(GLA_COLS - 1568)], axis=1)
    w_mla = jnp.concatenate([w[:, o[5]:o[8]], z(MLA_COLS - 576)], axis=1)
    w_diff = w[:, o[8]:o[11]]
    w_pool = w[:, o[11]:o[12]]
    return [a.astype(BF16) for a in (w_gla, w_mla, w_diff, w_pool)]


def _pack_w_uq(w):
    w = w.reshape(MLA_Q_RANK, MLA_H, MLA_QK)
    return jnp.concatenate([w[:, :, :MLA_NOPE].reshape(MLA_Q_RANK, -1),
                            w[:, :, MLA_NOPE:].reshape(MLA_Q_RANK, -1)], axis=1).astype(BF16)


def _pack_w_ukv(w):
    w = w.reshape(MLA_KV_RANK, MLA_H, MLA_NOPE + MLA_V)
    return jnp.concatenate([w[:, :, :MLA_NOPE].reshape(MLA_KV_RANK, -1),
                            w[:, :, MLA_NOPE:].reshape(MLA_KV_RANK, -1)], axis=1).astype(BF16)


def kernel(x_prompt, x_sample, state_gla, cache_mla, cache_diff_k, cache_diff_v, c, c_ctx,
           ada_w, ada_b, norm1_g, norm2_g, w_in, w_out, gla_gk_w, gla_gk_b, gla_onorm_g,
           mla_qnorm_g, mla_w_uq, mla_kvnorm_g, mla_w_ukv, mla_q_g, mla_k_g,
           diff_q_g, diff_k_g, diff_lambda, diff_onorm_g, pool_w, pool_scale,
           peer_wq, peer_keys, peer_u, peer_v):
    B, S, _ = x_prompt.shape
    DB, DS, _ = x_sample.shape
    L = ada_w.shape[0]
    t_ctx = B * S
    t_dec = DB * DS
    assert t_ctx % ROW_BLOCK == 0 and DS % ROW_BLOCK == 0 and t_ctx % DS == 0 and DB + 1 <= 16
    row = _mod_row_fn(t_ctx // ROW_BLOCK, DS // ROW_BLOCK)

    x = jnp.concatenate([x_prompt.reshape(t_ctx, D), x_sample.reshape(t_dec, D)], axis=0)
    cond = jnp.concatenate([c_ctx[None, :], c, jnp.zeros((15 - DB, D), F32)], axis=0)
    mod = _ada_mod(cond, ada_w, ada_b)
    tables = _rope_tables(DS)
    kc_all = cache_diff_k.reshape(DB, L, -1, DIFF_H * LANES)
    vc_all = cache_diff_v.reshape(DB, L, -1, DIFF_H * LANES)

    st_gla = st_mla = st_dk = st_dv = None
    mods = [mod[l].reshape(16, 1, 6 * D) for l in range(L)]
    h = _normmod(x, norm1_g[0], mods[0], row, 1, 0)
    dec_blk0 = t_ctx // DS
    for l in range(L):
        mod3 = mods[l]
        w_gla, w_mla, w_diff, w_pool = _pack_w_in(w_in[l])
        p_gla = _matmul(h, w_gla, name="in_gla")
        p_mla = _matmul(h, w_mla, name="in_mla")
        p_diff = _matmul(h, w_diff, name="in_diff")
        p_pool = _matmul(h, w_pool, name="in_pool")

        w_uq = _pack_w_uq(mla_w_uq[l])
        w_ukv = _pack_w_ukv(mla_w_ukv[l])
        gla_w = (gla_gk_w[l], gla_gk_b[l], gla_onorm_g[l])
        mla_w = (mla_qnorm_g[l], w_uq, mla_kvnorm_g[l], w_ukv, mla_q_g[l], mla_k_g[l])
        diff_w = (diff_q_g[l], diff_k_g[l], diff_lambda[l], diff_onorm_g[l])
        pool_wl = pool_w[l].astype(BF16)
        og, st_gla = _gla(p_gla, *gla_w, None, nb=B, S=S, blk0=0, o_prev=None,
                          st_prev=st_gla, lidx=l, n_layers=L)
        og, _ = _gla(p_gla, *gla_w, state_gla, nb=DB, S=DS, blk0=dec_blk0, o_prev=og,
                     st_prev=None, lidx=l, n_layers=L)
        om, st_mla = _mla(p_mla, *mla_w, None, None, nb=B, S=S, blk0=0, lidx=l,
                          o_prev=None, new_prev=st_mla, n_layers=L)
        om, _ = _mla(p_mla, *mla_w, cache_mla, tables, nb=DB, S=DS, blk0=dec_blk0, lidx=l,
                     o_prev=om, new_prev=None, n_layers=L)
        od, st_dk, st_dv = _diff(p_diff, *diff_w, None, None, None, nb=B, S=S, blk0=0, lidx=l,
                                 o_prev=None, k_prev=st_dk, v_prev=st_dv, n_layers=L)
        od, _, _ = _diff(p_diff, *diff_w, kc_all, vc_all, tables, nb=DB, S=DS, blk0=dec_blk0, lidx=l,
                         o_prev=od, k_prev=None, v_prev=None, n_layers=L)
        op = _pool(p_pool, pool_wl, pool_scale[l], nb=B, S=S, blk0=0, o_prev=None)
        op = _pool(p_pool, pool_wl, pool_scale[l], nb=DB, S=DS, blk0=dec_blk0, o_prev=op)

        x1, h2 = _outproj(og, om, od, op, w_out[l].astype(BF16), x, mod3, norm2_g[l], row)
        ii, jj, gg = _peer_topk(h2, peer_wq[l].astype(BF16), peer_keys[l].astype(BF16))
        ut = peer_u[l].reshape(PEER_EXPERTS // PEER_ET, PEER_ET, D).transpose(0, 2, 1).astype(BF16)
        peer = _peer_expert(h2, ii, jj, gg, ut, peer_v[l].astype(BF16))
        if l + 1 < L:
            x, h = _resid(x1, peer, mod3, row, (norm1_g[l + 1], mods[l + 1]))

    nc = t_ctx // ROW_BLOCK
    y_prompt, _ = _resid(x1, peer, mod3, row, blk0=0, nblk=nc)
    y_sample, _ = _resid(x1, peer, mod3, row, blk0=nc, nblk=t_dec // ROW_BLOCK)
    return (y_prompt.reshape(B, S, D), y_sample.reshape(DB, DS, D), st_gla, st_mla,
            st_dk.reshape(B, L, S, DIFF_H, 2 * DIFF_DQK), st_dv.reshape(B, L, S, DIFF_H, DIFF_DV))
```

```python
import functools
import math

import jax
import jax.numpy as jnp
import numpy as np
from jax import lax
from jax.experimental import pallas as pl
from jax.experimental.pallas import tpu as pltpu

F32 = jnp.float32
BF16 = jnp.bfloat16
HI = lax.Precision.HIGHEST

D = 2048
DEPTH = 2
GRID_W = 64
GROUP_W = 512
GLA_H, GLA_DK, GLA_DV, GLA_LR, GLA_NORM, GLA_CHUNK = 4, 64, 128, 16, 16.0, 64
MLA_H, MLA_NOPE, MLA_ROPE, MLA_V, MLA_QK = 4, 128, 64, 128, 192
MLA_Q_RANK, MLA_KV_RANK = 384, 128
DIFF_H, DIFF_DV, DIFF_DQK = 4, 128, 64
POOL_WINDOWS = (2, 4, 8, 16)
POOL_CH = 128
PEER_H, PEER_NKEYS, PEER_TOPK, PEER_DHALF = 8, 128, 16, 128
PEER_EXPERTS = PEER_NKEYS * PEER_NKEYS
ROPE_BASE = 10000.0
EPS = 1e-6

LANES = 128
VMEM_LIMIT = 56 * 1024 * 1024

GLA_COLS = 1664
MLA_COLS = 640
DIFF_COLS = 1536
POOL_COLS = 512

ROW_BLOCK = 256
PEER_TB = 512
PEER_ET = 1024
G_HALF = 64
G_PITCH = 72


def _cparams(sem):
    return pltpu.CompilerParams(dimension_semantics=sem, vmem_limit_bytes=VMEM_LIMIT)


def _dot(a, b, precision=None):
    return jnp.dot(a, b, preferred_element_type=F32, precision=precision)


def _dot_nt(a, b, precision=None):
    return lax.dot_general(a, b, (((1,), (1,)), ((), ())), preferred_element_type=F32,
                           precision=precision)


def _dot_tn(a, b, precision=None):
    return lax.dot_general(a, b, (((0,), (0,)), ((), ())), preferred_element_type=F32,
                           precision=precision)


def _silu(x):
    return x / (1.0 + jnp.exp(-x))


def _ada_kernel(c_ref, w_ref, b_ref, o_ref):
    a = _silu(c_ref[...]).astype(BF16)
    o_ref[0] = _dot(a, w_ref[0].astype(BF16)) + b_ref[0]


def _ada_mod(cond16, ada_w, ada_b):
    L, _, N = ada_w.shape
    tn = 1024
    return pl.pallas_call(
        _ada_kernel,
        out_shape=jax.ShapeDtypeStruct((L, 16, N), F32),
        grid=(L, N // tn),
        in_specs=[pl.BlockSpec((16, D), lambda l, j: (0, 0)),
                  pl.BlockSpec((1, D, tn), lambda l, j: (l, 0, j)),
                  pl.BlockSpec((1, 1, tn), lambda l, j: (l, 0, j))],
        out_specs=pl.BlockSpec((1, 16, tn), lambda l, j: (l, 0, j)),
        compiler_params=_cparams(("arbitrary", "arbitrary")),
        name="ada_mod",
    )(cond16, ada_w, ada_b.reshape(L, 1, N))


def _mod_row_fn(n_ctx_blocks, blocks_per_dec):
    def row(i):
        return jnp.where(i < n_ctx_blocks, 0, 1 + (i - n_ctx_blocks) // blocks_per_dec)
    return row


def _mod_spec(row, chunk):
    return pl.BlockSpec((1, 1, D), lambda i: (row(i), 0, chunk))


def _normmod_kernel(x_ref, g_ref, sc_ref, sh_ref, xo_ref, o_ref):
    x = x_ref[...]
    xo_ref[...] = x
    y = x * lax.rsqrt(jnp.mean(x * x, axis=-1, keepdims=True) + EPS) * g_ref[...]
    o_ref[...] = (y * (1.0 + sc_ref[0]) + sh_ref[0]).astype(BF16)


def _normmod_part(xp, g, mod3, row, blk0, T, x_prev, h_prev):
    mspec = lambda chunk: pl.BlockSpec((1, 1, D), lambda i: (row(blk0 + i), 0, chunk))
    ospec = pl.BlockSpec((ROW_BLOCK, D), lambda i: (blk0 + i, 0))
    return _carried_call(
        _normmod_kernel,
        grid=(xp.shape[0] // ROW_BLOCK,),
        in_specs=[pl.BlockSpec((ROW_BLOCK, D), lambda i: (i, 0)),
                  pl.BlockSpec((1, D), lambda i: (0, 0)), mspec(1), mspec(0)],
        args=[xp, g.reshape(1, D), mod3, mod3],
        outs=[(jax.ShapeDtypeStruct((T, D), F32), ospec, x_prev),
              (jax.ShapeDtypeStruct((T, D), BF16), ospec, h_prev)],
        scratch_shapes=[], name="normmod")


def _mm_kernel(h_ref, w_ref, o_ref):
    o_ref[...] = _dot(h_ref[...], w_ref[...]).astype(o_ref.dtype)


def _matmul(h, w, tm=512, out_dtype=F32, name="matmul"):
    T, K = h.shape
    N = w.shape[1]
    return pl.pallas_call(
        _mm_kernel,
        out_shape=jax.ShapeDtypeStruct((T, N), out_dtype),
        grid=(T // tm,),
        in_specs=[pl.BlockSpec((tm, K), lambda i: (i, 0)),
                  pl.BlockSpec((K, N), lambda i: (0, 0))],
        out_specs=pl.BlockSpec((tm, N), lambda i: (i, 0)),
        compiler_params=_cparams(("parallel",)),
        name=name,
    )(h, w)


def _gla_kernel(*refs, S, has_state, emit_state, unroll):
    if has_state:
        p_ref, gkw_ref, gkb_ref, on_ref, s0_ref = refs[:5]
        rest = refs[5:]
    else:
        p_ref, gkw_ref, gkb_ref, on_ref = refs[:4]
        s0_ref = None
        rest = refs[4:]
    if emit_state:
        o_ref, st_ref, lg_scr, of_scr, st_scr = rest
    else:
        o_ref, lg_scr, of_scr, st_scr = rest
        st_ref = None
    C = GLA_CHUNK
    n = S // C
    H, DK, DV = GLA_H, GLA_DK, GLA_DV
    HK = H * DK
    r_i = lax.broadcasted_iota(jnp.int32, (C, C), 0)
    c_i = lax.broadcasted_iota(jnp.int32, (C, C), 1)
    on_g = on_ref[...]

    def eye(m):
        return (lax.broadcasted_iota(jnp.int32, (m, m), 0)
                == lax.broadcasted_iota(jnp.int32, (m, m), 1)).astype(F32)

    for d in (0, 1):
        lr = p_ref[:, 1536 + GLA_LR * d:1536 + GLA_LR * (d + 1)]
        x = _dot(lr, gkw_ref[d], HI) + gkb_ref[d]
        lg_scr[...] = (jnp.minimum(x, 0.0) - jnp.log(1.0 + jnp.exp(-jnp.abs(x)))) * (1.0 / GLA_NORM)
        if d == 0:
            causal = r_i >= c_i
        else:
            causal = r_i <= c_i
        cmat = jnp.where(causal, 1.0, 0.0).astype(BF16)
        for h in range(H):
            if has_state:
                st_scr[h] = _dot_tn(s0_ref[0, 0, d, h], eye(DK), HI)
            else:
                st_scr[h] = jnp.zeros((DV, DK), F32)

        def chunk(ci, carry, d=d, cmat=cmat, causal=causal):
            c = ci if d == 0 else n - 1 - ci
            r0 = pl.multiple_of(c * C, C)
            gc = lg_scr[pl.ds(r0, C), :]
            g1 = gc.astype(BF16)
            rem = gc - g1.astype(F32)
            g2 = rem.astype(BF16)
            g3 = (rem - g2.astype(F32)).astype(BF16)
            cs = _dot(cmat, jnp.concatenate([g1, g2, g3], axis=1))
            cum = cs[:, :HK] + cs[:, HK:2 * HK] + cs[:, 2 * HK:]
            q = p_ref[pl.ds(r0, C), 0:256] * (DK ** -0.5)
            k = p_ref[pl.ds(r0, C), 256:512]
            qe = (q * jnp.exp(cum)).astype(BF16)
            ke = (k * jnp.exp(-cum)).astype(BF16)
            tot = jnp.sum(gc, axis=0, keepdims=True)
            kd = (k * jnp.exp(tot - cum)).astype(BF16)
            dec = jnp.exp(tot)
            outs = []
            for h in range(H):
                sl = slice(h * DK, (h + 1) * DK)
                v_h = p_ref[pl.ds(r0, C), 512 + h * DV:512 + (h + 1) * DV].astype(BF16)
                st = st_scr[h]
                att = _dot_nt(qe[:, sl], ke[:, sl])
                att = jnp.where(causal, att, 0.0).astype(BF16)
                o_h = _dot_nt(qe[:, sl], st.astype(BF16)) + _dot(att, v_h)
                st_scr[h] = dec[:, sl] * st + _dot_tn(v_h, kd[:, sl])
                outs.append(o_h)
            o_c = jnp.concatenate(outs, axis=1)
            if d == 0:
                of_scr[pl.ds(r0, C), :] = o_c
            else:
                o_c = o_c + of_scr[pl.ds(r0, C), :]
                gate = p_ref[pl.ds(r0, C), 1024:1536]
                res = []
                for h in range(H):
                    oh = o_c[:, h * DV:(h + 1) * DV]
                    oh = oh * lax.rsqrt(jnp.mean(oh * oh, axis=-1, keepdims=True) + EPS) * on_g
                    res.append(oh * _silu(gate[:, h * DV:(h + 1) * DV]))
                o_ref[pl.ds(r0, C), :] = jnp.concatenate(res, axis=1).astype(o_ref.dtype)
            return carry

        lax.fori_loop(0, n, chunk, 0, unroll=unroll)
        if emit_state:
            for h in range(H):
                st_ref[0, 0, d, h] = _dot_tn(st_scr[h], eye(DV), HI)


def _carried_call(kern, *, grid, in_specs, args, outs, scratch_shapes, name):
    n_in = len(args)
    carried = [(k, o[2]) for k, o in enumerate(outs) if o[2] is not None]
    n_carried = len(carried)

    def body(*refs):
        kern(*refs[:n_in], *refs[n_in + n_carried:])

    return pl.pallas_call(
        body,
        out_shape=[o[0] for o in outs], grid=grid,
        in_specs=list(in_specs) + [pl.BlockSpec(memory_space=pl.ANY)] * n_carried,
        out_specs=[o[1] for o in outs],
        scratch_shapes=scratch_shapes,
        input_output_aliases={n_in + j: k for j, (k, _) in enumerate(carried)},
        compiler_params=_cparams(("parallel",)),
        name=name,
    )(*args, *[prev for _, prev in carried])


def _gla(proj, gk_w, gk_b, onorm_g, s0, *, nb, S, blk0, o_prev, st_prev, lidx, n_layers):
    has_state = s0 is not None
    emit_state = not has_state
    in_specs = [pl.BlockSpec((S, GLA_COLS), lambda b: (blk0 + b, 0)),
                pl.BlockSpec((2, GLA_LR, GLA_H * GLA_DK), lambda b: (0, 0, 0)),
                pl.BlockSpec((2, 1, GLA_H * GLA_DK), lambda b: (0, 0, 0)),
                pl.BlockSpec((1, GLA_DV), lambda b: (0, 0))]
    args = [proj, gk_w, gk_b.reshape(2, 1, -1), onorm_g.reshape(1, -1)]
    st_dims = (2, GLA_H, GLA_DK, GLA_DV)
    if has_state:
        in_specs.append(pl.BlockSpec((1, 1) + st_dims, lambda b: (b, lidx, 0, 0, 0, 0)))
        args.append(s0)
    outs = [(jax.ShapeDtypeStruct((proj.shape[0], GLA_H * GLA_DV), BF16),
             pl.BlockSpec((S, GLA_H * GLA_DV), lambda b: (blk0 + b, 0)), o_prev)]
    if emit_state:
        outs.append((jax.ShapeDtypeStruct((nb, n_layers) + st_dims, F32),
                     pl.BlockSpec((1, 1) + st_dims, lambda b: (b, lidx, 0, 0, 0, 0)), st_prev))
    res = _carried_call(
        functools.partial(_gla_kernel, S=S, has_state=has_state, emit_state=emit_state, unroll=4),
        grid=(nb,), in_specs=in_specs, args=args, outs=outs,
        scratch_shapes=[pltpu.VMEM((S, GLA_H * GLA_DK), F32),
                        pltpu.VMEM((S, GLA_H * GLA_DV), F32),
                        pltpu.VMEM((GLA_H, GLA_DV, GLA_DK), F32)],
        name="gla_dec" if has_state else "gla_ctx")
    return res if emit_state else (res[0], None)


def _rope_tables(n_tokens):
    t = jnp.arange(n_tokens)
    row = (t // GRID_W).astype(F32)
    col = (t % GRID_W).astype(F32)
    half = 16
    freq = ROPE_BASE ** (-jnp.arange(half, dtype=F32) / half)
    ar = row[:, None] * freq[None, :]
    ac = col[:, None] * freq[None, :]
    cos = jnp.concatenate([jnp.cos(ar), jnp.cos(ar), jnp.cos(ac), jnp.cos(ac)], axis=1)
    sin = jnp.concatenate([-jnp.sin(ar), jnp.sin(ar), -jnp.sin(ac), jnp.sin(ac)], axis=1)
    return jnp.tile(cos, (1, 2)), jnp.tile(sin, (1, 2))


def _rope128(x, cos, sin):
    lane = lax.broadcasted_iota(jnp.int32, x.shape, 1)
    first = (lane % 32) < 16
    partner = jnp.where(first, pltpu.roll(x, LANES - 16, 1), pltpu.roll(x, 16, 1))
    return x * cos + partner * sin


def _mla_kernel(*refs, S, P, rope, qb):
    if P:
        (p_ref, qn_ref, wuq_ref, kvn_ref, wukv_ref, qg_ref, kg_ref, cache_ref, cos_ref, sin_ref,
         o_ref, k_scr, v_scr) = refs
        new_ref = None
    else:
        (p_ref, qn_ref, wuq_ref, kvn_ref, wukv_ref, qg_ref, kg_ref,
         o_ref, new_ref, k_scr, v_scr) = refs
    H = MLA_H
    NK = P + S
    qg = qg_ref[...]
    kg = kg_ref[...]
    qg_n, qg_r = qg[:, :MLA_NOPE], qg[:, MLA_NOPE:]
    kg_n, kg_r = kg[:, :MLA_NOPE], kg[:, MLA_NOPE:]
    qg_r2 = jnp.concatenate([qg_r, qg_r], axis=1)
    kg_r2 = jnp.concatenate([kg_r, kg_r], axis=1)

    ckv = p_ref[:, 384:512]
    ckv = ckv * lax.rsqrt(jnp.mean(ckv * ckv, axis=-1, keepdims=True) + EPS) * kvn_ref[...]
    kpe = p_ref[:, 512:576]
    if new_ref is not None:
        new_ref[0, 0, :, 0:MLA_KV_RANK] = ckv
        new_ref[0, 0, :, MLA_KV_RANK:] = kpe

    def expand(c, pe, do_rope, r0, nrows):
        kv = _dot(c.astype(BF16), wukv_ref[...])
        pe2 = jnp.concatenate([pe, pe], axis=1)
        pe_ss = jnp.sum(pe * pe, axis=-1, keepdims=True)
        pe_g = pe2 * kg_r2
        if do_rope:
            pe_g = _rope128(pe_g, cos_ref[...], sin_ref[...])
        for h in range(H):
            kn = kv[:, h * MLA_NOPE:(h + 1) * MLA_NOPE]
            r = lax.rsqrt((jnp.sum(kn * kn, axis=-1, keepdims=True) + pe_ss) / MLA_QK + EPS)
            kfull = jnp.concatenate([kn * kg_n * r, pe_g[:, :MLA_ROPE] * r], axis=1)
            k_scr[h, r0:r0 + nrows, :] = kfull.astype(BF16)
            v_scr[h, r0:r0 + nrows, :] = kv[:, 512 + h * MLA_V:512 + (h + 1) * MLA_V].astype(BF16)

    if P:
        cc = cache_ref[0, 0]
        expand(cc[:, :MLA_KV_RANK], cc[:, MLA_KV_RANK:], False, 0, P)
    expand(ckv, kpe, rope, P, S)

    cq = p_ref[:, 0:384]
    cq = cq * lax.rsqrt(jnp.mean(cq * cq, axis=-1, keepdims=True) + EPS) * qn_ref[...]
    qall = _dot(cq.astype(BF16), wuq_ref[...])
    scale = MLA_QK ** -0.5
    lane = lax.broadcasted_iota(jnp.int32, (S, LANES), 1)
    for hp in range(H // 2):
        rs = []
        qns = []
        for j in range(2):
            h = 2 * hp + j
            qn = qall[:, h * MLA_NOPE:(h + 1) * MLA_NOPE]
            qr = qall[:, 512 + h * MLA_ROPE:512 + (h + 1) * MLA_ROPE]
            ss = jnp.sum(qn * qn, axis=-1, keepdims=True) + jnp.sum(qr * qr, axis=-1, keepdims=True)
            r = lax.rsqrt(ss / MLA_QK + EPS)
            rs.append(r)
            qns.append(qn * qg_n * r)
        qr2 = qall[:, 512 + hp * LANES:512 + (hp + 1) * LANES] * qg_r2
        qr2 = qr2 * jnp.where(lane < MLA_ROPE, rs[0], rs[1])
        if rope:
            qr2 = _rope128(qr2, cos_ref[...], sin_ref[...])
        for j in range(2):
            h = 2 * hp + j
            qfull = jnp.concatenate([qns[j], qr2[:, j * MLA_ROPE:(j + 1) * MLA_ROPE]], axis=1)
            qfull = (qfull * scale).astype(BF16)
            kh = k_scr[h]
            vh = v_scr[h]
            for i in range(S // qb):
                sc = _dot_nt(qfull[i * qb:(i + 1) * qb], kh)
                m = jnp.max(sc, axis=-1, keepdims=True)
                e = jnp.exp(sc - m)
                o = _dot(e.astype(BF16), vh) / jnp.sum(e, axis=-1, keepdims=True)
                o_ref[i * qb:(i + 1) * qb, h * MLA_V:(h + 1) * MLA_V] = o.astype(o_ref.dtype)


def _mla(proj, qnorm_g, w_uq, kvnorm_g, w_ukv, q_g, k_g, cache, tables, *, nb, S, blk0, lidx,
         o_prev, new_prev, n_layers):
    P = 0 if cache is None else cache.shape[2]
    const2 = lambda b: (0, 0)
    in_specs = [pl.BlockSpec((S, MLA_COLS), lambda b: (blk0 + b, 0)),
                pl.BlockSpec((1, MLA_Q_RANK), const2),
                pl.BlockSpec((MLA_Q_RANK, MLA_H * MLA_QK), const2),
                pl.BlockSpec((1, MLA_KV_RANK), const2),
                pl.BlockSpec((MLA_KV_RANK, MLA_H * (MLA_NOPE + MLA_V)), const2),
                pl.BlockSpec((1, MLA_QK), const2),
                pl.BlockSpec((1, MLA_QK), const2)]
    args = [proj, qnorm_g.reshape(1, -1), w_uq, kvnorm_g.reshape(1, -1), w_ukv,
            q_g.reshape(1, -1), k_g.reshape(1, -1)]
    if P:
        in_specs += [pl.BlockSpec((1, 1, P, MLA_KV_RANK + MLA_ROPE), lambda b: (b, lidx, 0, 0)),
                     pl.BlockSpec((S, LANES), const2), pl.BlockSpec((S, LANES), const2)]
        args += [cache, tables[0], tables[1]]
    outs = [(jax.ShapeDtypeStruct((proj.shape[0], MLA_H * MLA_V), BF16),
             pl.BlockSpec((S, MLA_H * MLA_V), lambda b: (blk0 + b, 0)), o_prev)]
    if not P:
        cw = MLA_KV_RANK + MLA_ROPE
        outs.append((jax.ShapeDtypeStruct((nb, n_layers, S, cw), F32),
                     pl.BlockSpec((1, 1, S, cw), lambda b: (b, lidx, 0, 0)), new_prev))
    res = _carried_call(
        functools.partial(_mla_kernel, S=S, P=P, rope=bool(P), qb=min(S, 256)),
        grid=(nb,), in_specs=in_specs, args=args, outs=outs,
        scratch_shapes=[pltpu.VMEM((MLA_H, P + S, MLA_QK), BF16),
                        pltpu.VMEM((MLA_H, P + S, MLA_V), BF16)],
        name="mla_dec" if P else "mla_ctx")
    return (res[0], None) if P else res


def _group_rms64(x, g2):
    lane = lax.broadcasted_iota(jnp.int32, x.shape, 1)
    lo = lane < DIFF_DQK
    xx = x * x
    s_all = jnp.sum(xx, axis=-1, keepdims=True)
    s_lo = jnp.sum(jnp.where(lo, xx, 0.0), axis=-1, keepdims=True)
    ms = jnp.where(lo, s_lo, s_all - s_lo) * (1.0 / DIFF_DQK)
    return x * lax.rsqrt(ms + EPS) * g2


def _diff_kernel(*refs, S, P, rope, qb, lam_init):
    if P:
        (p_ref, qg_ref, kg_ref, lam_ref, on_ref, kc_ref, vc_ref, cos_ref, sin_ref,
         o_ref, k_scr, v_scr) = refs
        ko_ref = vo_ref = None
    else:
        (p_ref, qg_ref, kg_ref, lam_ref, on_ref, o_ref, ko_ref, vo_ref, k_scr, v_scr) = refs
    H = DIFF_H
    qg2 = jnp.concatenate([qg_ref[...], qg_ref[...]], axis=1)
    kg2 = jnp.concatenate([kg_ref[...], kg_ref[...]], axis=1)
    lp = lam_ref[...]
    l1 = jnp.sum(jnp.sum(lp[0:1] * lp[1:2], axis=-1, keepdims=True), axis=0, keepdims=True)
    l2 = jnp.sum(jnp.sum(lp[2:3] * lp[3:4], axis=-1, keepdims=True), axis=0, keepdims=True)
    lam = jnp.exp(l1) - jnp.exp(l2) + lam_init
    scale = DIFF_DQK ** -0.5
    lane = lax.broadcasted_iota(jnp.int32, (S, LANES), 1)
    lo = lane < DIFF_DQK
    on_g = on_ref[...]

    if vo_ref is not None:
        vo_ref[0, 0] = p_ref[:, 1024:1536]
    for h in range(H):
        sl = slice(h * LANES, (h + 1) * LANES)
        kn = _group_rms64(p_ref[:, 512 + h * LANES:512 + (h + 1) * LANES], kg2)
        if ko_ref is not None:
            ko_ref[0, 0, :, sl] = kn
        if rope:
            kn = _rope128(kn, cos_ref[...], sin_ref[...])
        if P:
            k_scr[0:P, :] = kc_ref[0, 0, :, sl].astype(BF16)
            v_scr[0:P, :] = vc_ref[0, 0, :, sl].astype(BF16)
        k_scr[P:P + S, :] = kn.astype(BF16)
        v_scr[P:P + S, :] = p_ref[:, 1024 + h * LANES:1024 + (h + 1) * LANES].astype(BF16)
        qn = _group_rms64(p_ref[:, sl], qg2)
        if rope:
            qn = _rope128(qn, cos_ref[...], sin_ref[...])
        qn = qn * scale
        q0 = jnp.where(lo, qn, 0.0).astype(BF16)
        q1 = jnp.where(lo, 0.0, qn).astype(BF16)
        kh = k_scr[...]
        vh = v_scr[...]
        for i in range(S // qb):
            rows = slice(i * qb, (i + 1) * qb)
            es, ss = [], []
            for qq in (q0, q1):
                sc = _dot_nt(qq[rows], kh)
                m = jnp.max(sc, axis=-1, keepdims=True)
                e = jnp.exp(sc - m)
                es.append(e)
                ss.append(jnp.sum(e, axis=-1, keepdims=True))
            a = (es[0] - (lam * ss[0] / ss[1]) * es[1]).astype(BF16)
            o = _dot(a, vh) / ss[0]
            o = o * lax.rsqrt(jnp.mean(o * o, axis=-1, keepdims=True) + EPS) * on_g
            o_ref[rows, sl] = (o * (1.0 - lam_init)).astype(o_ref.dtype)


def _diff(proj, q_g, k_g, lam_p, onorm_g, kc, vc, tables, *, nb, S, blk0, lidx,
          o_prev, k_prev, v_prev, n_layers):
    P = 0 if kc is None else kc.shape[2]
    const2 = lambda b: (0, 0)
    in_specs = [pl.BlockSpec((S, DIFF_COLS), lambda b: (blk0 + b, 0)),
                pl.BlockSpec((1, DIFF_DQK), const2), pl.BlockSpec((1, DIFF_DQK), const2),
                pl.BlockSpec((4, DIFF_DQK), const2), pl.BlockSpec((1, DIFF_DV), const2)]
    args = [proj, q_g.reshape(1, -1), k_g.reshape(1, -1), lam_p, onorm_g.reshape(1, -1)]
    if P:
        cspec = pl.BlockSpec((1, 1, P, DIFF_H * LANES), lambda b: (b, lidx, 0, 0))
        in_specs += [cspec, cspec, pl.BlockSpec((S, LANES), const2), pl.BlockSpec((S, LANES), const2)]
        args += [kc, vc, tables[0], tables[1]]
    lam_init = 0.8 - 0.6 * math.exp(-0.3 * lidx)
    outs = [(jax.ShapeDtypeStruct((proj.shape[0], DIFF_H * DIFF_DV), BF16),
             pl.BlockSpec((S, DIFF_H * DIFF_DV), lambda b: (blk0 + b, 0)), o_prev)]
    if not P:
        cshape = jax.ShapeDtypeStruct((nb, n_layers, S, DIFF_H * LANES), F32)
        cspec = pl.BlockSpec((1, 1, S, DIFF_H * LANES), lambda b: (b, lidx, 0, 0))
        outs += [(cshape, cspec, k_prev), (cshape, cspec, v_prev)]
    res = _carried_call(
        functools.partial(_diff_kernel, S=S, P=P, rope=bool(P), qb=min(S, 256), lam_init=lam_init),
        grid=(nb,), in_specs=in_specs, args=args, outs=outs,
        scratch_shapes=[pltpu.VMEM((P + S, LANES), BF16), pltpu.VMEM((P + S, LANES), BF16)],
        name="diff_dec" if P else "diff_ctx")
    return (res[0], None, None) if P else res


POOL_PAD = 8


def _pool_kernel(p_ref, w_ref, sc_ref, o_ref, pad_scr, *, S):
    zeros = jnp.zeros((POOL_PAD, GROUP_W), F32)
    pad_scr[0:POOL_PAD, :] = zeros
    pad_scr[POOL_PAD + S:POOL_PAD + S + POOL_PAD, :] = zeros
    pad_scr[POOL_PAD:POOL_PAD + S, :] = p_ref[...]
    t = lax.broadcasted_iota(jnp.int32, (S, POOL_CH), 0)
    for gi, win in enumerate(POOL_WINDOWS):
        cs = slice(gi * POOL_CH, (gi + 1) * POOL_CH)
        acc = None
        for dlt in range(-(win // 2), win // 2):
            piece = pad_scr[POOL_PAD + dlt:POOL_PAD + dlt + S, cs]
            acc = piece if acc is None else acc + piece
        cnt = (jnp.minimum(t + win // 2, S) - jnp.maximum(t - win // 2, 0)).astype(F32)
        y = (acc / cnt - p_ref[:, cs]).astype(BF16)
        o_ref[:, cs] = (_dot(y, w_ref[gi]) * sc_ref[:, cs]).astype(o_ref.dtype)


def _pool(proj, w, scale, *, nb, S, blk0, o_prev):
    return _carried_call(
        functools.partial(_pool_kernel, S=S),
        grid=(nb,),
        in_specs=[pl.BlockSpec((S, POOL_COLS), lambda b: (blk0 + b, 0)),
                  pl.BlockSpec((len(POOL_WINDOWS), POOL_CH, POOL_CH), lambda b: (0, 0, 0)),
                  pl.BlockSpec((1, GROUP_W), lambda b: (0, 0))],
        args=[proj, w, scale.reshape(1, -1)],
        outs=[(jax.ShapeDtypeStruct((proj.shape[0], GROUP_W), BF16),
               pl.BlockSpec((S, GROUP_W), lambda b: (blk0 + b, 0)), o_prev)],
        scratch_shapes=[pltpu.VMEM((S + 2 * POOL_PAD, GROUP_W), F32)],
        name="pool")[0]


def _outproj_kernel(og_ref, om_ref, od_ref, op_ref, w_ref, x_ref, ga_ref, g2_ref, sc_ref, sh_ref,
                    x1_ref, h2_ref):
    mix = _dot(og_ref[...], w_ref[0:512, :])
    mix += _dot(om_ref[...], w_ref[512:1024, :])
    mix += _dot(od_ref[...], w_ref[1024:1536, :])
    mix += _dot(op_ref[...], w_ref[1536:2048, :])
    x1 = x_ref[...] + ga_ref[0] * mix
    x1_ref[...] = x1
    y = x1 * lax.rsqrt(jnp.mean(x1 * x1, axis=-1, keepdims=True) + EPS) * g2_ref[...]
    h2_ref[...] = (y * (1.0 + sc_ref[0]) + sh_ref[0]).astype(BF16)


def _outproj(og, om, od, op, w_out, x, mod3, norm2_g, row):
    T = x.shape[0]
    ospec = pl.BlockSpec((ROW_BLOCK, GROUP_W), lambda i: (i, 0))
    xspec = pl.BlockSpec((ROW_BLOCK, D), lambda i: (i, 0))
    return pl.pallas_call(
        _outproj_kernel,
        out_shape=[jax.ShapeDtypeStruct((T, D), F32), jax.ShapeDtypeStruct((T, D), BF16)],
        grid=(T // ROW_BLOCK,),
        in_specs=[ospec, ospec, ospec, ospec,
                  pl.BlockSpec((D, D), lambda i: (0, 0)),
                  xspec, _mod_spec(row, 2),
                  pl.BlockSpec((1, D), lambda i: (0, 0)),
                  _mod_spec(row, 4), _mod_spec(row, 3)],
        out_specs=[xspec, xspec],
        compiler_params=_cparams(("parallel",)),
        name="outproj",
    )(og, om, od, op, w_out, x, mod3, norm2_g.reshape(1, D), mod3, mod3)


def _topk_chains(x_refs, nrows, tt):
    rows = lax.broadcasted_iota(jnp.int32, (nrows, tt), 0)
    krow = lax.broadcasted_iota(jnp.int32, (PEER_TOPK, tt), 0)

    def body(k, carry):
        out = []
        for x_ref, (vals, sel) in zip(x_refs, carry):
            x = x_ref[...]
            m = jnp.max(x, axis=0, keepdims=True)
            pos = jnp.argmax(x, axis=0, keepdims=True).astype(jnp.int32)
            x_ref[...] = jnp.where(rows == pos, -jnp.inf, x)
            out.append((jnp.where(krow == k, m, vals), jnp.where(krow == k, pos, sel)))
        return tuple(out)

    init = (jnp.zeros((PEER_TOPK, tt), F32), jnp.zeros((PEER_TOPK, tt), jnp.int32))
    return lax.fori_loop(0, PEER_TOPK, body, tuple(init for _ in x_refs))


CAND_ROWS = 80


def _cand_ab(r):
    a = jnp.where(r < 16, 0, jnp.where(r < 72, 1 + ((r - 16) >> 3), r - 64))
    b = jnp.where(r < 16, r, jnp.where(r < 72, (r - 16) & 7, 0))
    return a, b


def _peer_topk_kernel(h_ref, wq_ref, keys_ref, i_ref, j_ref, g_ref,
                      q_scr, sc_scr, cand_scr, i_scr, j_scr, g_scr, *, tt):
    q = _dot(h_ref[...], wq_ref[...]).astype(BF16)
    for hp in range(2 * PEER_H):
        q_scr[hp] = q[:, hp * PEER_DHALF:(hp + 1) * PEER_DHALF]
    K = PEER_TOPK

    def head_pair(hh, carry):
        for u in range(2):
            for p in range(2):
                h = 2 * hh + u
                sc_scr[2 * u + p] = _dot_nt(keys_ref[h, p], q_scr[2 * h + p])
        tops = _topk_chains([sc_scr.at[c] for c in range(4)], PEER_NKEYS, tt)
        for u in range(2):
            (s0, _), (s1, _) = tops[2 * u], tops[2 * u + 1]
            cand_scr[u, 0:K, :] = s0[0:1, :] + s1
            for a in range(1, 8):
                cand_scr[u, 8 + 8 * a:16 + 8 * a, :] = s0[a:a + 1, :] + s1[0:8, :]
            cand_scr[u, 72:80, :] = s0[8:16, :] + s1[0:1, :]
        picks = _topk_chains([cand_scr.at[u] for u in range(2)], CAND_ROWS, tt)
        for u in range(2):
            (_, i0), (_, i1) = tops[2 * u], tops[2 * u + 1]
            best, pos = picks[u]
            a_sel, b_sel = _cand_ab(pos)
            i_sel = jnp.zeros((K, tt), jnp.int32)
            j_sel = jnp.zeros((K, tt), jnp.int32)
            for a in range(K):
                i_sel = jnp.where(a_sel == a, i0[a:a + 1, :], i_sel)
                j_sel = jnp.where(b_sel == a, i1[a:a + 1, :], j_sel)
            e = jnp.exp(best - best[0:1, :])
            i_scr[2 * hh + u] = i_sel.astype(F32)
            j_scr[2 * hh + u] = j_sel.astype(F32)
            g_scr[2 * hh + u] = e / jnp.sum(e, axis=0, keepdims=True)
        return carry

    lax.fori_loop(0, PEER_H // 2, head_pair, 0)
    nrow = PEER_H * K
    i_ref[...] = i_scr[...].reshape(nrow, tt).T.astype(jnp.int32)
    j_ref[...] = j_scr[...].reshape(nrow, tt).T.astype(jnp.int32)
    g_ref[...] = g_scr[...].reshape(nrow, tt).T


def _peer_topk(h2, wq, keys, tt=ROW_BLOCK):
    T = h2.shape[0]
    npick = PEER_H * PEER_TOPK
    ospec = pl.BlockSpec((tt, npick), lambda i: (i, 0))
    return pl.pallas_call(
        functools.partial(_peer_topk_kernel, tt=tt),
        out_shape=[jax.ShapeDtypeStruct((T, npick), jnp.int32),
                   jax.ShapeDtypeStruct((T, npick), jnp.int32),
                   jax.ShapeDtypeStruct((T, npick), F32)],
        grid=(T // tt,),
        in_specs=[pl.BlockSpec((tt, D), lambda i: (i, 0)),
                  pl.BlockSpec((D, D), lambda i: (0, 0)),
                  pl.BlockSpec((PEER_H, 2, PEER_NKEYS, PEER_DHALF), lambda i: (0, 0, 0, 0))],
        out_specs=[ospec, ospec, ospec],
        scratch_shapes=[pltpu.VMEM((2 * PEER_H, tt, PEER_DHALF), BF16),
                        pltpu.VMEM((4, PEER_NKEYS, tt), F32),
                        pltpu.VMEM((2, CAND_ROWS, tt), F32),
                        pltpu.VMEM((PEER_H, PEER_TOPK, tt), F32),
                        pltpu.VMEM((PEER_H, PEER_TOPK, tt), F32),
                        pltpu.VMEM((PEER_H, PEER_TOPK, tt), F32)],
        compiler_params=_cparams(("parallel",)),
        name="peer_topk",
    )(h2, wq, keys)


def _gelu_tanh(x):
    return 0.5 * x * (1.0 + jnp.tanh(0.7978845608028654 * (x + 0.044715 * x * x * x)))


def _peer_expert_kernel(h_ref, i_ref, j_ref, g_ref, ut_ref, v_ref, o_ref, gate_scr, *, tb, et):
    e = pl.program_id(1)
    nk = PEER_NKEYS
    npick = PEER_H * PEER_TOPK
    tiles = et // nk

    @pl.when(e == 0)
    def _():
        o_ref[...] = jnp.zeros_like(o_ref)
        sub = lax.broadcasted_iota(jnp.int32, (nk, npick), 0)
        zero = jnp.zeros((nk, npick), BF16)

        def pair(tp, carry):
            lhs, rhs = [], []
            for u in range(2):
                t = 2 * tp + u
                irow = i_ref[pl.ds(t, 1), :]
                jrow = j_ref[pl.ds(t, 1), :]
                grow = g_ref[pl.ds(t, 1), :]
                lhs.append(jnp.where(sub == irow, 1.0, 0.0).astype(BF16))
                xt = jnp.where(sub == jrow, grow, 0.0).astype(BF16)
                rhs.append(jnp.concatenate([xt, zero] if u == 0 else [zero, xt], axis=1))
            g2 = _dot_nt(jnp.concatenate(lhs, axis=1), jnp.concatenate(rhs, axis=0))
            for u in range(2):
                g = g2[:, u * nk:(u + 1) * nk].astype(BF16).astype(F32)
                hi = pltpu.bitcast(g[:G_HALF], jnp.uint32)
                lo = pltpu.bitcast(g[G_HALF:], jnp.uint32)
                r0 = pl.multiple_of((2 * tp + u) * G_PITCH, 8)
                gate_scr[pl.ds(r0, G_HALF), :] = hi | (lo >> 16)
            return carry

        lax.fori_loop(0, tb // 2, pair, 0, unroll=16)

    s = _dot(h_ref[...], ut_ref[0])
    i0 = e * tiles
    upper = i0 < G_HALF
    r0 = jnp.where(upper, i0, i0 - G_HALF)
    words = jnp.concatenate(
        [gate_scr[pl.ds(r0 + il, tb, stride=G_PITCH), :] for il in range(tiles)], axis=1)
    bits = jnp.where(upper, words & jnp.uint32(0xFFFF0000), words << 16)
    c = (pltpu.bitcast(bits, F32) * _gelu_tanh(s)).astype(BF16)
    o_ref[...] += _dot(c, v_ref[...])


def _peer_expert(h2, ii, jj, gg, ut, v, tb=PEER_TB, et=PEER_ET):
    T = h2.shape[0]
    ne = PEER_EXPERTS // et
    assert G_HALF % (et // PEER_NKEYS) == 0
    npick = PEER_H * PEER_TOPK
    tspec = pl.BlockSpec((tb, npick), lambda i, e: (i, 0))
    xspec = pl.BlockSpec((tb, D), lambda i, e: (i, 0))
    return pl.pallas_call(
        functools.partial(_peer_expert_kernel, tb=tb, et=et),
        out_shape=jax.ShapeDtypeStruct((T, D), F32),
        grid=(T // tb, ne),
        in_specs=[xspec, tspec, tspec, tspec,
                  pl.BlockSpec((1, D, et), lambda i, e: (e, 0, 0)),
                  pl.BlockSpec((et, D), lambda i, e: (e, 0))],
        out_specs=xspec,
        scratch_shapes=[pltpu.VMEM((tb * G_PITCH, LANES), jnp.uint32)],
        compiler_params=_cparams(("parallel", "arbitrary")),
        name="peer_expert",
    )(h2, ii, jj, gg, ut, v)


def _resid_kernel(*refs, emit_h):
    if emit_h:
        x1_ref, p_ref, gf_ref, g_ref, sc_ref, sh_ref, x_ref, h_ref = refs
    else:
        x1_ref, p_ref, gf_ref, x_ref = refs
    x = x1_ref[...] + gf_ref[0] * p_ref[...]
    x_ref[...] = x
    if emit_h:
        y = x * lax.rsqrt(jnp.mean(x * x, axis=-1, keepdims=True) + EPS) * g_ref[...]
        h_ref[...] = (y * (1.0 + sc_ref[0]) + sh_ref[0]).astype(BF16)


def _resid(x1, peer, mod3, row, next_norm=None, blk0=0, nblk=None):
    nblk = x1.shape[0] // ROW_BLOCK if nblk is None else nblk
    ispec = pl.BlockSpec((ROW_BLOCK, D), lambda i: (blk0 + i, 0))
    ospec = pl.BlockSpec((ROW_BLOCK, D), lambda i: (i, 0))
    mspec = lambda chunk: pl.BlockSpec((1, 1, D), lambda i: (row(blk0 + i), 0, chunk))
    in_specs = [ispec, ispec, mspec(5)]
    args = [x1, peer, mod3]
    out_shape = [jax.ShapeDtypeStruct((nblk * ROW_BLOCK, D), F32)]
    out_specs = [ospec]
    if next_norm is not None:
        g, mod3n = next_norm
        in_specs += [pl.BlockSpec((1, D), lambda i: (0, 0)), mspec(1), mspec(0)]
        args += [g.reshape(1, D), mod3n, mod3n]
        out_shape.append(jax.ShapeDtypeStruct((nblk * ROW_BLOCK, D), BF16))
        out_specs.append(ospec)
    res = pl.pallas_call(
        functools.partial(_resid_kernel, emit_h=next_norm is not None),
        out_shape=out_shape, grid=(nblk,), in_specs=in_specs, out_specs=out_specs,
        compiler_params=_cparams(("parallel",)),
        name="resid",
    )(*args)
    return res if next_norm is not None else (res[0], None)


def _pack_w_in(w):
    o = np.cumsum([0, 256, 256, 512, 512, 32, 384, 128, 64, 512, 512, 512, 512]).tolist()
    z = lambda n: jnp.zeros((D, n), w.dtype)
    w_gla = jnp.concatenate([w[:, o[0]:o[5]], z(GLA_COLS - 1568)], axis=1)
    w_mla = jnp.concatenate([w[:, o[5]:o[8]], z(MLA_COLS - 576)], axis=1)
    w_diff = w[:, o[8]:o[11]]
    w_pool = w[:, o[11]:o[12]]
    return [a.astype(BF16) for a in (w_gla, w_mla, w_diff, w_pool)]


def _pack_w_uq(w):
    w = w.reshape(MLA_Q_RANK, MLA_H, MLA_QK)
    return jnp.concatenate([w[:, :, :MLA_NOPE].reshape(MLA_Q_RANK, -1),
                            w[:, :, MLA_NOPE:].reshape(MLA_Q_RANK, -1)], axis=1).astype(BF16)


def _pack_w_ukv(w):
    w = w.reshape(MLA_KV_RANK, MLA_H, MLA_NOPE + MLA_V)
    return jnp.concatenate([w[:, :, :MLA_NOPE].reshape(MLA_KV_RANK, -1),
                            w[:, :, MLA_NOPE:].reshape(MLA_KV_RANK, -1)], axis=1).astype(BF16)


def kernel(x_prompt, x_sample, state_gla, cache_mla, cache_diff_k, cache_diff_v, c, c_ctx,
           ada_w, ada_b, norm1_g, norm2_g, w_in, w_out, gla_gk_w, gla_gk_b, gla_onorm_g,
           mla_qnorm_g, mla_w_uq, mla_kvnorm_g, mla_w_ukv, mla_q_g, mla_k_g,
           diff_q_g, diff_k_g, diff_lambda, diff_onorm_g, pool_w, pool_scale,
           peer_wq, peer_keys, peer_u, peer_v):
    B, S, _ = x_prompt.shape
    DB, DS, _ = x_sample.shape
    L = ada_w.shape[0]
    t_ctx = B * S
    t_dec = DB * DS
    assert t_ctx % ROW_BLOCK == 0 and DS % ROW_BLOCK == 0 and t_ctx % DS == 0 and DB + 1 <= 16
    row = _mod_row_fn(t_ctx // ROW_BLOCK, DS // ROW_BLOCK)

    cond = jnp.concatenate([c_ctx[None, :], c, jnp.zeros((15 - DB, D), F32)], axis=0)
    mod = _ada_mod(cond, ada_w, ada_b)
    tables = _rope_tables(DS)
    kc_all = cache_diff_k.reshape(DB, L, -1, DIFF_H * LANES)
    vc_all = cache_diff_v.reshape(DB, L, -1, DIFF_H * LANES)

    st_gla = st_mla = st_dk = st_dv = None
    mods = [mod[l].reshape(16, 1, 6 * D) for l in range(L)]
    T = t_ctx + t_dec
    x, h = _normmod_part(x_prompt.reshape(t_ctx, D), norm1_g[0], mods[0], row, 0, T, None, None)
    x, h = _normmod_part(x_sample.reshape(t_dec, D), norm1_g[0], mods[0], row,
                         t_ctx // ROW_BLOCK, T, x, h)
    dec_blk0 = t_ctx // DS
    for l in range(L):
        mod3 = mods[l]
        w_gla, w_mla, w_diff, w_pool = _pack_w_in(w_in[l])
        p_gla = _matmul(h, w_gla, name="in_gla")
        p_mla = _matmul(h, w_mla, name="in_mla")
        p_diff = _matmul(h, w_diff, name="in_diff")
        p_pool = _matmul(h, w_pool, name="in_pool")

        w_uq = _pack_w_uq(mla_w_uq[l])
        w_ukv = _pack_w_ukv(mla_w_ukv[l])
        gla_w = (gla_gk_w[l], gla_gk_b[l], gla_onorm_g[l])
        mla_w = (mla_qnorm_g[l], w_uq, mla_kvnorm_g[l], w_ukv, mla_q_g[l], mla_k_g[l])
        diff_w = (diff_q_g[l], diff_k_g[l], diff_lambda[l], diff_onorm_g[l])
        pool_wl = pool_w[l].astype(BF16)
        og, st_gla = _gla(p_gla, *gla_w, None, nb=B, S=S, blk0=0, o_prev=None,
                          st_prev=st_gla, lidx=l, n_layers=L)
        og, _ = _gla(p_gla, *gla_w, state_gla, nb=DB, S=DS, blk0=dec_blk0, o_prev=og,
                     st_prev=None, lidx=l, n_layers=L)
        om, st_mla = _mla(p_mla, *mla_w, None, None, nb=B, S=S, blk0=0, lidx=l,
                          o_prev=None, new_prev=st_mla, n_layers=L)
        om, _ = _mla(p_mla, *mla_w, cache_mla, tables, nb=DB, S=DS, blk0=dec_blk0, lidx=l,
                     o_prev=om, new_prev=None, n_layers=L)
        od, st_dk, st_dv = _diff(p_diff, *diff_w, None, None, None, nb=B, S=S, blk0=0, lidx=l,
                                 o_prev=None, k_prev=st_dk, v_prev=st_dv, n_layers=L)
        od, _, _ = _diff(p_diff, *diff_w, kc_all, vc_all, tables, nb=DB, S=DS, blk0=dec_blk0, lidx=l,
                         o_prev=od, k_prev=None, v_prev=None, n_layers=L)
        op = _pool(p_pool, pool_wl, pool_scale[l], nb=B, S=S, blk0=0, o_prev=None)
        op = _pool(p_pool, pool_wl, pool_scale[l], nb=DB, S=DS, blk0=dec_blk0, o_prev=op)

        x1, h2 = _outproj(og, om, od, op, w_out[l].astype(BF16), x, mod3, norm2_g[l], row)
        ii, jj, gg = _peer_topk(h2, peer_wq[l].astype(BF16), peer_keys[l].astype(BF16))
        ut = peer_u[l].reshape(PEER_EXPERTS // PEER_ET, PEER_ET, D).transpose(0, 2, 1).astype(BF16)
        peer = _peer_expert(h2, ii, jj, gg, ut, peer_v[l].astype(BF16))
        if l + 1 < L:
            x, h = _resid(x1, peer, mod3, row, (norm1_g[l + 1], mods[l + 1]))

    nc = t_ctx // ROW_BLOCK
    y_prompt, _ = _resid(x1, peer, mod3, row, blk0=0, nblk=nc)
    y_sample, _ = _resid(x1, peer, mod3, row, blk0=nc, nblk=t_dec // ROW_BLOCK)
    return (y_prompt.reshape(B, S, D), y_sample.reshape(DB, DS, D), st_gla, st_mla,
            st_dk.reshape(B, L, S, DIFF_H, 2 * DIFF_DQK), st_dv.reshape(B, L, S, DIFF_H, DIFF_DV))
```

```python
import functools
import math

import jax
import jax.numpy as jnp
import numpy as np
from jax import lax
from jax.experimental import pallas as pl
from jax.experimental.pallas import tpu as pltpu

F32 = jnp.float32
BF16 = jnp.bfloat16
HI = lax.Precision.HIGHEST

D = 2048
DEPTH = 2
GRID_W = 64
GROUP_W = 512
GLA_H, GLA_DK, GLA_DV, GLA_LR, GLA_NORM, GLA_CHUNK = 4, 64, 128, 16, 16.0, 64
MLA_H, MLA_NOPE, MLA_ROPE, MLA_V, MLA_QK = 4, 128, 64, 128, 192
MLA_Q_RANK, MLA_KV_RANK = 384, 128
DIFF_H, DIFF_DV, DIFF_DQK = 4, 128, 64
POOL_WINDOWS = (2, 4, 8, 16)
POOL_CH = 128
PEER_H, PEER_NKEYS, PEER_TOPK, PEER_DHALF = 8, 128, 16, 128
PEER_EXPERTS = PEER_NKEYS * PEER_NKEYS
ROPE_BASE = 10000.0
EPS = 1e-6

LANES = 128
VMEM_LIMIT = 56 * 1024 * 1024

GLA_COLS = 1664
MLA_COLS = 640
DIFF_COLS = 1536
POOL_COLS = 512

ROW_BLOCK = 256
GLA_NSEQ = 1
PEER_TB = 512
PEER_ET = 1024
PEER_SPLIT = 2
G_HALF = 64
G_PITCH = 72


def _cparams(sem):
    return pltpu.CompilerParams(dimension_semantics=sem, vmem_limit_bytes=VMEM_LIMIT)


def _dot(a, b, precision=None):
    return jnp.dot(a, b, preferred_element_type=F32, precision=precision)


def _dot_nt(a, b, precision=None):
    return lax.dot_general(a, b, (((1,), (1,)), ((), ())), preferred_element_type=F32,
                           precision=precision)


def _dot_tn(a, b, precision=None):
    return lax.dot_general(a, b, (((0,), (0,)), ((), ())), preferred_element_type=F32,
                           precision=precision)


def _silu(x):
    return x / (1.0 + jnp.exp(-x))


def _ada_kernel(c_ref, w_ref, b_ref, o_ref):
    a = _silu(c_ref[...]).astype(BF16)
    o_ref[0] = _dot(a, w_ref[0].astype(BF16)) + b_ref[0]


def _ada_mod(cond16, ada_w, ada_b):
    L, _, N = ada_w.shape
    tn = 1024
    return pl.pallas_call(
        _ada_kernel,
        out_shape=jax.ShapeDtypeStruct((L, 16, N), F32),
        grid=(L, N // tn),
        in_specs=[pl.BlockSpec((16, D), lambda l, j: (0, 0)),
                  pl.BlockSpec((1, D, tn), lambda l, j: (l, 0, j)),
                  pl.BlockSpec((1, 1, tn), lambda l, j: (l, 0, j))],
        out_specs=pl.BlockSpec((1, 16, tn), lambda l, j: (l, 0, j)),
        compiler_params=_cparams(("arbitrary", "arbitrary")),
        name="ada_mod",
    )(cond16, ada_w, ada_b.reshape(L, 1, N))


def _mod_row_fn(n_ctx_blocks, blocks_per_dec):
    def row(i):
        return jnp.where(i < n_ctx_blocks, 0, 1 + (i - n_ctx_blocks) // blocks_per_dec)
    return row


def _mod_spec(row, chunk):
    return pl.BlockSpec((1, 1, D), lambda i: (row(i), 0, chunk))


def _normmod_kernel(x_ref, g_ref, sc_ref, sh_ref, xo_ref, o_ref):
    x = x_ref[...]
    xo_ref[...] = x
    y = x * lax.rsqrt(jnp.mean(x * x, axis=-1, keepdims=True) + EPS) * g_ref[...]
    o_ref[...] = (y * (1.0 + sc_ref[0]) + sh_ref[0]).astype(BF16)


def _normmod_part(xp, g, mod3, row, blk0, T, x_prev, h_prev):
    mspec = lambda chunk: pl.BlockSpec((1, 1, D), lambda i: (row(blk0 + i), 0, chunk))
    ospec = pl.BlockSpec((ROW_BLOCK, D), lambda i: (blk0 + i, 0))
    return _carried_call(
        _normmod_kernel,
        grid=(xp.shape[0] // ROW_BLOCK,),
        in_specs=[pl.BlockSpec((ROW_BLOCK, D), lambda i: (i, 0)),
                  pl.BlockSpec((1, D), lambda i: (0, 0)), mspec(1), mspec(0)],
        args=[xp, g.reshape(1, D), mod3, mod3],
        outs=[(jax.ShapeDtypeStruct((T, D), F32), ospec, x_prev),
              (jax.ShapeDtypeStruct((T, D), BF16), ospec, h_prev)],
        scratch_shapes=[], name="normmod")


def _mm_kernel(h_ref, w_ref, o_ref):
    o_ref[...] = _dot(h_ref[...], w_ref[...]).astype(o_ref.dtype)


def _matmul(h, w, tm=512, out_dtype=F32, name="matmul"):
    T, K = h.shape
    N = w.shape[1]
    return pl.pallas_call(
        _mm_kernel,
        out_shape=jax.ShapeDtypeStruct((T, N), out_dtype),
        grid=(T // tm,),
        in_specs=[pl.BlockSpec((tm, K), lambda i: (i, 0)),
                  pl.BlockSpec((K, N), lambda i: (0, 0))],
        out_specs=pl.BlockSpec((tm, N), lambda i: (i, 0)),
        compiler_params=_cparams(("parallel",)),
        name=name,
    )(h, w)


def _gla_kernel(*refs, S, has_state, emit_state, unroll):
    if has_state:
        p_ref, gkw_ref, gkb_ref, on_ref, s0_ref = refs[:5]
        rest = refs[5:]
    else:
        p_ref, gkw_ref, gkb_ref, on_ref = refs[:4]
        s0_ref = None
        rest = refs[4:]
    if emit_state:
        o_ref, st_ref, lg_scr, of_scr, st_scr = rest
    else:
        o_ref, lg_scr, of_scr, st_scr = rest
        st_ref = None
    C = GLA_CHUNK
    n = S // C
    H, DK, DV = GLA_H, GLA_DK, GLA_DV
    HK = H * DK
    r_i = lax.broadcasted_iota(jnp.int32, (C, C), 0)
    c_i = lax.broadcasted_iota(jnp.int32, (C, C), 1)
    on_g = on_ref[...]

    for d in (0, 1):
        lr = p_ref[:, 1536 + GLA_LR * d:1536 + GLA_LR * (d + 1)]
        x = _dot(lr, gkw_ref[d], HI) + gkb_ref[d]
        lg_scr[...] = (jnp.minimum(x, 0.0) - jnp.log(1.0 + jnp.exp(-jnp.abs(x)))) * (1.0 / GLA_NORM)
        if d == 0:
            causal = r_i >= c_i
        else:
            causal = r_i <= c_i
        cmat = jnp.where(causal, 1.0, 0.0).astype(BF16)
        for s in range(GLA_NSEQ):
            for h in range(H):
                if has_state:
                    st_scr[s * H + h] = s0_ref[s, 0, d, h]
                else:
                    st_scr[s * H + h] = jnp.zeros((DV, DK), F32)

        def chunk(ci, carry, d=d, cmat=cmat, causal=causal):
            c = ci if d == 0 else n - 1 - ci
            for s in range(GLA_NSEQ):
                r0 = pl.multiple_of(s * S + c * C, C)
                gc = lg_scr[pl.ds(r0, C), :]
                g1 = gc.astype(BF16)
                rem = gc - g1.astype(F32)
                g2 = rem.astype(BF16)
                g3 = (rem - g2.astype(F32)).astype(BF16)
                cs = _dot(cmat, jnp.concatenate([g1, g2, g3], axis=1))
                cum = cs[:, :HK] + cs[:, HK:2 * HK] + cs[:, 2 * HK:]
                q = p_ref[pl.ds(r0, C), 0:256] * (DK ** -0.5)
                k = p_ref[pl.ds(r0, C), 256:512]
                qe = (q * jnp.exp(cum)).astype(BF16)
                ke = (k * jnp.exp(-cum)).astype(BF16)
                tot = jnp.sum(gc, axis=0, keepdims=True)
                kd = (k * jnp.exp(tot - cum)).astype(BF16)
                dec = jnp.exp(tot)
                outs = []
                for h in range(H):
                    sl = slice(h * DK, (h + 1) * DK)
                    v_h = p_ref[pl.ds(r0, C), 512 + h * DV:512 + (h + 1) * DV].astype(BF16)
                    st = st_scr[s * H + h]
                    att = _dot_nt(qe[:, sl], ke[:, sl])
                    att = jnp.where(causal, att, 0.0).astype(BF16)
                    o_h = _dot_nt(qe[:, sl], st.astype(BF16)) + _dot(att, v_h)
                    st_scr[s * H + h] = dec[:, sl] * st + _dot_tn(v_h, kd[:, sl])
                    outs.append(o_h)
                o_c = jnp.concatenate(outs, axis=1)
                if d == 0:
                    of_scr[pl.ds(r0, C), :] = o_c
                else:
                    o_c = o_c + of_scr[pl.ds(r0, C), :]
                    gate = p_ref[pl.ds(r0, C), 1024:1536]
                    res = []
                    for h in range(H):
                        oh = o_c[:, h * DV:(h + 1) * DV]
                        oh = oh * lax.rsqrt(jnp.mean(oh * oh, axis=-1, keepdims=True) + EPS) * on_g
                        res.append(oh * _silu(gate[:, h * DV:(h + 1) * DV]))
                    o_ref[pl.ds(r0, C), :] = jnp.concatenate(res, axis=1).astype(o_ref.dtype)
            return carry

        lax.fori_loop(0, n, chunk, 0, unroll=unroll)
        if emit_state:
            for s in range(GLA_NSEQ):
                for h in range(H):
                    st_ref[s, 0, d, h] = st_scr[s * H + h]


def _carried_call(kern, *, grid, in_specs, args, outs, scratch_shapes, name):
    n_in = len(args)
    carried = [(k, o[2]) for k, o in enumerate(outs) if o[2] is not None]
    n_carried = len(carried)

    def body(*refs):
        kern(*refs[:n_in], *refs[n_in + n_carried:])

    return pl.pallas_call(
        body,
        out_shape=[o[0] for o in outs], grid=grid,
        in_specs=list(in_specs) + [pl.BlockSpec(memory_space=pl.ANY)] * n_carried,
        out_specs=[o[1] for o in outs],
        scratch_shapes=scratch_shapes,
        input_output_aliases={n_in + j: k for j, (k, _) in enumerate(carried)},
        compiler_params=_cparams(("parallel",)),
        name=name,
    )(*args, *[prev for _, prev in carried])


def _gla(proj, gk_w, gk_b, onorm_g, s0, *, nb, S, blk0, o_prev, st_prev, lidx, n_layers):
    has_state = s0 is not None
    emit_state = not has_state
    ns = GLA_NSEQ
    assert nb % ns == 0 and blk0 % ns == 0
    gb0 = blk0 // ns
    rows = ns * S
    in_specs = [pl.BlockSpec((rows, GLA_COLS), lambda b: (gb0 + b, 0)),
                pl.BlockSpec((2, GLA_LR, GLA_H * GLA_DK), lambda b: (0, 0, 0)),
                pl.BlockSpec((2, 1, GLA_H * GLA_DK), lambda b: (0, 0, 0)),
                pl.BlockSpec((1, GLA_DV), lambda b: (0, 0))]
    args = [proj, gk_w, gk_b.reshape(2, 1, -1), onorm_g.reshape(1, -1)]
    st_dims = (2, GLA_H, GLA_DV, GLA_DK)
    if has_state:
        in_specs.append(pl.BlockSpec((ns, 1) + st_dims, lambda b: (b, lidx, 0, 0, 0, 0)))
        args.append(s0)
    outs = [(jax.ShapeDtypeStruct((proj.shape[0], GLA_H * GLA_DV), BF16),
             pl.BlockSpec((rows, GLA_H * GLA_DV), lambda b: (gb0 + b, 0)), o_prev)]
    if emit_state:
        outs.append((jax.ShapeDtypeStruct((nb, n_layers) + st_dims, F32),
                     pl.BlockSpec((ns, 1) + st_dims, lambda b: (b, lidx, 0, 0, 0, 0)), st_prev))
    res = _carried_call(
        functools.partial(_gla_kernel, S=S, has_state=has_state, emit_state=emit_state,
                          unroll=4 // ns),
        grid=(nb // ns,), in_specs=in_specs, args=args, outs=outs,
        scratch_shapes=[pltpu.VMEM((rows, GLA_H * GLA_DK), F32),
                        pltpu.VMEM((rows, GLA_H * GLA_DV), F32),
                        pltpu.VMEM((ns * GLA_H, GLA_DV, GLA_DK), F32)],
        name="gla_dec" if has_state else "gla_ctx")
    return res if emit_state else (res[0], None)


def _rope_tables(n_tokens):
    t = jnp.arange(n_tokens)
    row = (t // GRID_W).astype(F32)
    col = (t % GRID_W).astype(F32)
    half = 16
    freq = ROPE_BASE ** (-jnp.arange(half, dtype=F32) / half)
    ar = row[:, None] * freq[None, :]
    ac = col[:, None] * freq[None, :]
    cos = jnp.concatenate([jnp.cos(ar), jnp.cos(ar), jnp.cos(ac), jnp.cos(ac)], axis=1)
    sin = jnp.concatenate([-jnp.sin(ar), jnp.sin(ar), -jnp.sin(ac), jnp.sin(ac)], axis=1)
    return jnp.tile(cos, (1, 2)), jnp.tile(sin, (1, 2))


def _rope128(x, cos, sin):
    lane = lax.broadcasted_iota(jnp.int32, x.shape, 1)
    first = (lane % 32) < 16
    partner = jnp.where(first, pltpu.roll(x, LANES - 16, 1), pltpu.roll(x, 16, 1))
    return x * cos + partner * sin


def _mla_kernel(*refs, S, P, rope, qb):
    if P:
        (p_ref, qn_ref, wuq_ref, kvn_ref, wukv_ref, qg_ref, kg_ref, cache_ref, cos_ref, sin_ref,
         o_ref, k_scr, v_scr) = refs
        new_ref = None
    else:
        (p_ref, qn_ref, wuq_ref, kvn_ref, wukv_ref, qg_ref, kg_ref,
         o_ref, new_ref, k_scr, v_scr) = refs
    H = MLA_H
    NK = P + S
    qg = qg_ref[...]
    kg = kg_ref[...]
    qg_n, qg_r = qg[:, :MLA_NOPE], qg[:, MLA_NOPE:]
    kg_n, kg_r = kg[:, :MLA_NOPE], kg[:, MLA_NOPE:]
    qg_r2 = jnp.concatenate([qg_r, qg_r], axis=1)
    kg_r2 = jnp.concatenate([kg_r, kg_r], axis=1)

    ckv = p_ref[:, 384:512]
    ckv = ckv * lax.rsqrt(jnp.mean(ckv * ckv, axis=-1, keepdims=True) + EPS) * kvn_ref[...]
    kpe = p_ref[:, 512:576]
    if new_ref is not None:
        new_ref[0, 0, :, 0:MLA_KV_RANK] = ckv
        new_ref[0, 0, :, MLA_KV_RANK:] = kpe

    def expand(c, pe, do_rope, r0, nrows):
        kv = _dot(c.astype(BF16), wukv_ref[...])
        pe2 = jnp.concatenate([pe, pe], axis=1)
        pe_ss = jnp.sum(pe * pe, axis=-1, keepdims=True)
        pe_g = pe2 * kg_r2
        if do_rope:
            pe_g = _rope128(pe_g, cos_ref[...], sin_ref[...])
        for h in range(H):
            kn = kv[:, h * MLA_NOPE:(h + 1) * MLA_NOPE]
            r = lax.rsqrt((jnp.sum(kn * kn, axis=-1, keepdims=True) + pe_ss) / MLA_QK + EPS)
            kfull = jnp.concatenate([kn * kg_n * r, pe_g[:, :MLA_ROPE] * r], axis=1)
            k_scr[h, r0:r0 + nrows, :] = kfull.astype(BF16)
            v_scr[h, r0:r0 + nrows, :] = kv[:, 512 + h * MLA_V:512 + (h + 1) * MLA_V].astype(BF16)

    if P:
        cc = cache_ref[0, 0]
        expand(cc[:, :MLA_KV_RANK], cc[:, MLA_KV_RANK:], False, 0, P)
    expand(ckv, kpe, rope, P, S)

    cq = p_ref[:, 0:384]
    cq = cq * lax.rsqrt(jnp.mean(cq * cq, axis=-1, keepdims=True) + EPS) * qn_ref[...]
    qall = _dot(cq.astype(BF16), wuq_ref[...])
    scale = MLA_QK ** -0.5
    lane = lax.broadcasted_iota(jnp.int32, (S, LANES), 1)
    for hp in range(H // 2):
        rs = []
        qns = []
        for j in range(2):
            h = 2 * hp + j
            qn = qall[:, h * MLA_NOPE:(h + 1) * MLA_NOPE]
            qr = qall[:, 512 + h * MLA_ROPE:512 + (h + 1) * MLA_ROPE]
            ss = jnp.sum(qn * qn, axis=-1, keepdims=True) + jnp.sum(qr * qr, axis=-1, keepdims=True)
            r = lax.rsqrt(ss / MLA_QK + EPS)
            rs.append(r)
            qns.append(qn * qg_n * r)
        qr2 = qall[:, 512 + hp * LANES:512 + (hp + 1) * LANES] * qg_r2
        qr2 = qr2 * jnp.where(lane < MLA_ROPE, rs[0], rs[1])
        if rope:
            qr2 = _rope128(qr2, cos_ref[...], sin_ref[...])
        for j in range(2):
            h = 2 * hp + j
            qfull = jnp.concatenate([qns[j], qr2[:, j * MLA_ROPE:(j + 1) * MLA_ROPE]], axis=1)
            qfull = (qfull * scale).astype(BF16)
            kh = k_scr[h]
            vh = v_scr[h]
            for i in range(S // qb):
                sc = _dot_nt(qfull[i * qb:(i + 1) * qb], kh)
                m = jnp.max(sc, axis=-1, keepdims=True)
                e = jnp.exp(sc - m)
                o = _dot(e.astype(BF16), vh) / jnp.sum(e, axis=-1, keepdims=True)
                o_ref[i * qb:(i + 1) * qb, h * MLA_V:(h + 1) * MLA_V] = o.astype(o_ref.dtype)


def _mla(proj, qnorm_g, w_uq, kvnorm_g, w_ukv, q_g, k_g, cache, tables, *, nb, S, blk0, lidx,
         o_prev, new_prev, n_layers):
    P = 0 if cache is None else cache.shape[2]
    const2 = lambda b: (0, 0)
    in_specs = [pl.BlockSpec((S, MLA_COLS), lambda b: (blk0 + b, 0)),
                pl.BlockSpec((1, MLA_Q_RANK), const2),
                pl.BlockSpec((MLA_Q_RANK, MLA_H * MLA_QK), const2),
                pl.BlockSpec((1, MLA_KV_RANK), const2),
                pl.BlockSpec((MLA_KV_RANK, MLA_H * (MLA_NOPE + MLA_V)), const2),
                pl.BlockSpec((1, MLA_QK), const2),
                pl.BlockSpec((1, MLA_QK), const2)]
    args = [proj, qnorm_g.reshape(1, -1), w_uq, kvnorm_g.reshape(1, -1), w_ukv,
            q_g.reshape(1, -1), k_g.reshape(1, -1)]
    if P:
        in_specs += [pl.BlockSpec((1, 1, P, MLA_KV_RANK + MLA_ROPE), lambda b: (b, lidx, 0, 0)),
                     pl.BlockSpec((S, LANES), const2), pl.BlockSpec((S, LANES), const2)]
        args += [cache, tables[0], tables[1]]
    outs = [(jax.ShapeDtypeStruct((proj.shape[0], MLA_H * MLA_V), BF16),
             pl.BlockSpec((S, MLA_H * MLA_V), lambda b: (blk0 + b, 0)), o_prev)]
    if not P:
        cw = MLA_KV_RANK + MLA_ROPE
        outs.append((jax.ShapeDtypeStruct((nb, n_layers, S, cw), F32),
                     pl.BlockSpec((1, 1, S, cw), lambda b: (b, lidx, 0, 0)), new_prev))
    res = _carried_call(
        functools.partial(_mla_kernel, S=S, P=P, rope=bool(P), qb=min(S, 256)),
        grid=(nb,), in_specs=in_specs, args=args, outs=outs,
        scratch_shapes=[pltpu.VMEM((MLA_H, P + S, MLA_QK), BF16),
                        pltpu.VMEM((MLA_H, P + S, MLA_V), BF16)],
        name="mla_dec" if P else "mla_ctx")
    return (res[0], None) if P else res


def _group_rms64(x, g2):
    lane = lax.broadcasted_iota(jnp.int32, x.shape, 1)
    lo = lane < DIFF_DQK
    xx = x * x
    s_all = jnp.sum(xx, axis=-1, keepdims=True)
    s_lo = jnp.sum(jnp.where(lo, xx, 0.0), axis=-1, keepdims=True)
    ms = jnp.where(lo, s_lo, s_all - s_lo) * (1.0 / DIFF_DQK)
    return x * lax.rsqrt(ms + EPS) * g2


def _diff_kernel(*refs, S, P, rope, qb, lam_init):
    if P:
        (p_ref, qg_ref, kg_ref, lam_ref, on_ref, kc_ref, vc_ref, cos_ref, sin_ref,
         o_ref, k_scr, v_scr) = refs
        ko_ref = vo_ref = None
    else:
        (p_ref, qg_ref, kg_ref, lam_ref, on_ref, o_ref, ko_ref, vo_ref, k_scr, v_scr) = refs
    H = DIFF_H
    qg2 = jnp.concatenate([qg_ref[...], qg_ref[...]], axis=1)
    kg2 = jnp.concatenate([kg_ref[...], kg_ref[...]], axis=1)
    lp = lam_ref[...]
    l1 = jnp.sum(jnp.sum(lp[0:1] * lp[1:2], axis=-1, keepdims=True), axis=0, keepdims=True)
    l2 = jnp.sum(jnp.sum(lp[2:3] * lp[3:4], axis=-1, keepdims=True), axis=0, keepdims=True)
    lam = jnp.exp(l1) - jnp.exp(l2) + lam_init
    scale = DIFF_DQK ** -0.5
    lane = lax.broadcasted_iota(jnp.int32, (S, LANES), 1)
    lo = lane < DIFF_DQK
    on_g = on_ref[...]

    if vo_ref is not None:
        vo_ref[0, 0] = p_ref[:, 1024:1536]
    for h in range(H):
        sl = slice(h * LANES, (h + 1) * LANES)
        kn = _group_rms64(p_ref[:, 512 + h * LANES:512 + (h + 1) * LANES], kg2)
        if ko_ref is not None:
            ko_ref[0, 0, :, sl] = kn
        if rope:
            kn = _rope128(kn, cos_ref[...], sin_ref[...])
        if P:
            k_scr[0:P, :] = kc_ref[0, 0, :, sl].astype(BF16)
            v_scr[0:P, :] = vc_ref[0, 0, :, sl].astype(BF16)
        k_scr[P:P + S, :] = kn.astype(BF16)
        v_scr[P:P + S, :] = p_ref[:, 1024 + h * LANES:1024 + (h + 1) * LANES].astype(BF16)
        qn = _group_rms64(p_ref[:, sl], qg2)
        if rope:
            qn = _rope128(qn, cos_ref[...], sin_ref[...])
        qn = qn * scale
        q0 = jnp.where(lo, qn, 0.0).astype(BF16)
        q1 = jnp.where(lo, 0.0, qn).astype(BF16)
        kh = k_scr[...]
        vh = v_scr[...]
        for i in range(S // qb):
            rows = slice(i * qb, (i + 1) * qb)
            ps = []
            for qq in (q0, q1):
                sc = _dot_nt(qq[rows], kh)
                m = jnp.max(sc, axis=-1, keepdims=True)
                e = jnp.exp(sc - m)
                ps.append(e / jnp.sum(e, axis=-1, keepdims=True))
            a = (ps[0] - lam * ps[1]).astype(BF16)
            o = _dot(a, vh)
            o = o * lax.rsqrt(jnp.mean(o * o, axis=-1, keepdims=True) + EPS) * on_g
            o_ref[rows, sl] = (o * (1.0 - lam_init)).astype(o_ref.dtype)


def _diff(proj, q_g, k_g, lam_p, onorm_g, kc, vc, tables, *, nb, S, blk0, lidx,
          o_prev, k_prev, v_prev, n_layers):
    P = 0 if kc is None else kc.shape[2]
    const2 = lambda b: (0, 0)
    in_specs = [pl.BlockSpec((S, DIFF_COLS), lambda b: (blk0 + b, 0)),
                pl.BlockSpec((1, DIFF_DQK), const2), pl.BlockSpec((1, DIFF_DQK), const2),
                pl.BlockSpec((4, DIFF_DQK), const2), pl.BlockSpec((1, DIFF_DV), const2)]
    args = [proj, q_g.reshape(1, -1), k_g.reshape(1, -1), lam_p, onorm_g.reshape(1, -1)]
    if P:
        cspec = pl.BlockSpec((1, 1, P, DIFF_H * LANES), lambda b: (b, lidx, 0, 0))
        in_specs += [cspec, cspec, pl.BlockSpec((S, LANES), const2), pl.BlockSpec((S, LANES), const2)]
        args += [kc, vc, tables[0], tables[1]]
    lam_init = 0.8 - 0.6 * math.exp(-0.3 * lidx)
    outs = [(jax.ShapeDtypeStruct((proj.shape[0], DIFF_H * DIFF_DV), BF16),
             pl.BlockSpec((S, DIFF_H * DIFF_DV), lambda b: (blk0 + b, 0)), o_prev)]
    if not P:
        cshape = jax.ShapeDtypeStruct((nb, n_layers, S, DIFF_H * LANES), F32)
        cspec = pl.BlockSpec((1, 1, S, DIFF_H * LANES), lambda b: (b, lidx, 0, 0))
        outs += [(cshape, cspec, k_prev), (cshape, cspec, v_prev)]
    res = _carried_call(
        functools.partial(_diff_kernel, S=S, P=P, rope=bool(P), qb=min(S, 256), lam_init=lam_init),
        grid=(nb,), in_specs=in_specs, args=args, outs=outs,
        scratch_shapes=[pltpu.VMEM((P + S, LANES), BF16), pltpu.VMEM((P + S, LANES), BF16)],
        name="diff_dec" if P else "diff_ctx")
    return (res[0], None, None) if P else res


POOL_PAD = 8


def _pool_kernel(p_ref, w_ref, sc_ref, o_ref, pad_scr, *, S):
    zeros = jnp.zeros((POOL_PAD, GROUP_W), F32)
    pad_scr[0:POOL_PAD, :] = zeros
    pad_scr[POOL_PAD + S:POOL_PAD + S + POOL_PAD, :] = zeros
    pad_scr[POOL_PAD:POOL_PAD + S, :] = p_ref[...]
    t = lax.broadcasted_iota(jnp.int32, (S, POOL_CH), 0)
    for gi, win in enumerate(POOL_WINDOWS):
        cs = slice(gi * POOL_CH, (gi + 1) * POOL_CH)
        acc = None
        for dlt in range(-(win // 2), win // 2):
            piece = pad_scr[POOL_PAD + dlt:POOL_PAD + dlt + S, cs]
            acc = piece if acc is None else acc + piece
        cnt = (jnp.minimum(t + win // 2, S) - jnp.maximum(t - win // 2, 0)).astype(F32)
        y = (acc / cnt - p_ref[:, cs]).astype(BF16)
        o_ref[:, cs] = (_dot(y, w_ref[gi]) * sc_ref[:, cs]).astype(o_ref.dtype)


def _pool(proj, w, scale, *, nb, S, blk0, o_prev):
    return _carried_call(
        functools.partial(_pool_kernel, S=S),
        grid=(nb,),
        in_specs=[pl.BlockSpec((S, POOL_COLS), lambda b: (blk0 + b, 0)),
                  pl.BlockSpec((len(POOL_WINDOWS), POOL_CH, POOL_CH), lambda b: (0, 0, 0)),
                  pl.BlockSpec((1, GROUP_W), lambda b: (0, 0))],
        args=[proj, w, scale.reshape(1, -1)],
        outs=[(jax.ShapeDtypeStruct((proj.shape[0], GROUP_W), BF16),
               pl.BlockSpec((S, GROUP_W), lambda b: (blk0 + b, 0)), o_prev)],
        scratch_shapes=[pltpu.VMEM((S + 2 * POOL_PAD, GROUP_W), F32)],
        name="pool")[0]


def _outproj_kernel(og_ref, om_ref, od_ref, op_ref, w_ref, x_ref, ga_ref, g2_ref, sc_ref, sh_ref,
                    x1_ref, h2_ref):
    mix = _dot(og_ref[...], w_ref[0:512, :])
    mix += _dot(om_ref[...], w_ref[512:1024, :])
    mix += _dot(od_ref[...], w_ref[1024:1536, :])
    mix += _dot(op_ref[...], w_ref[1536:2048, :])
    x1 = x_ref[...] + ga_ref[0] * mix
    x1_ref[...] = x1
    y = x1 * lax.rsqrt(jnp.mean(x1 * x1, axis=-1, keepdims=True) + EPS) * g2_ref[...]
    h2_ref[...] = (y * (1.0 + sc_ref[0]) + sh_ref[0]).astype(BF16)


def _outproj(og, om, od, op, w_out, x, mod3, norm2_g, row):
    T = x.shape[0]
    ospec = pl.BlockSpec((ROW_BLOCK, GROUP_W), lambda i: (i, 0))
    xspec = pl.BlockSpec((ROW_BLOCK, D), lambda i: (i, 0))
    return pl.pallas_call(
        _outproj_kernel,
        out_shape=[jax.ShapeDtypeStruct((T, D), F32), jax.ShapeDtypeStruct((T, D), BF16)],
        grid=(T // ROW_BLOCK,),
        in_specs=[ospec, ospec, ospec, ospec,
                  pl.BlockSpec((D, D), lambda i: (0, 0)),
                  xspec, _mod_spec(row, 2),
                  pl.BlockSpec((1, D), lambda i: (0, 0)),
                  _mod_spec(row, 4), _mod_spec(row, 3)],
        out_specs=[xspec, xspec],
        compiler_params=_cparams(("parallel",)),
        name="outproj",
    )(og, om, od, op, w_out, x, mod3, norm2_g.reshape(1, D), mod3, mod3)


def _topk_chains(x_refs, nrows, tt):
    rows = lax.broadcasted_iota(jnp.int32, (nrows, tt), 0)
    krow = lax.broadcasted_iota(jnp.int32, (PEER_TOPK, tt), 0)
    ng = nrows // 8
    rows_f = rows.astype(F32)
    sub = lax.broadcasted_iota(jnp.int32, (8, tt), 0).astype(F32)
    row_groups = [sub + float(8 * g) for g in range(ng)]

    def argmax_rows(x):
        items = [(x[8 * g:8 * (g + 1)], row_groups[g]) for g in range(ng)]
        while len(items) > 1:
            nxt = []
            for j in range(0, len(items) - 1, 2):
                (a, ra), (b, rb) = items[j], items[j + 1]
                gt = b > a
                nxt.append((jnp.where(gt, b, a), jnp.where(gt, rb, ra)))
            if len(items) % 2:
                nxt.append(items[-1])
            items = nxt
        v, r = items[0]
        m = jnp.max(v, axis=0, keepdims=True)
        pos = jnp.min(jnp.where(v == m, r, float(nrows)), axis=0, keepdims=True)
        return m, pos

    def body(k, carry):
        out = []
        for x_ref, (vals, sel) in zip(x_refs, carry):
            x = x_ref[...]
            m, pos = argmax_rows(x)
            x_ref[...] = jnp.where(rows_f == pos, -jnp.inf, x)
            out.append((jnp.where(krow == k, m, vals), jnp.where(krow == k, pos, sel)))
        return tuple(out)

    init = (jnp.zeros((PEER_TOPK, tt), F32), jnp.zeros((PEER_TOPK, tt), F32))
    res = lax.fori_loop(0, PEER_TOPK, body, tuple(init for _ in x_refs))
    return [(vals, sel.astype(jnp.int32)) for vals, sel in res]


CAND_ROWS = 80


def _cand_ab(r):
    a = jnp.where(r < 16, 0, jnp.where(r < 72, 1 + ((r - 16) >> 3), r - 64))
    b = jnp.where(r < 16, r, jnp.where(r < 72, (r - 16) & 7, 0))
    return a, b


def _peer_topk_kernel(h_ref, wq_ref, keys_ref, i_ref, j_ref, g_ref,
                      q_scr, sc_scr, cand_scr, i_scr, j_scr, g_scr, *, tt):
    q = _dot(h_ref[...], wq_ref[...]).astype(BF16)
    for hp in range(2 * PEER_H):
        q_scr[hp] = q[:, hp * PEER_DHALF:(hp + 1) * PEER_DHALF]
    K = PEER_TOPK

    def head_pair(hh, carry):
        for u in range(2):
            for p in range(2):
                h = 2 * hh + u
                sc_scr[2 * u + p] = _dot_nt(keys_ref[h, p], q_scr[2 * h + p])
        tops = _topk_chains([sc_scr.at[c] for c in range(4)], PEER_NKEYS, tt)
        for u in range(2):
            (s0, _), (s1, _) = tops[2 * u], tops[2 * u + 1]
            cand_scr[u, 0:K, :] = s0[0:1, :] + s1
            for a in range(1, 8):
                cand_scr[u, 8 + 8 * a:16 + 8 * a, :] = s0[a:a + 1, :] + s1[0:8, :]
            cand_scr[u, 72:80, :] = s0[8:16, :] + s1[0:1, :]
        picks = _topk_chains([cand_scr.at[u] for u in range(2)], CAND_ROWS, tt)
        for u in range(2):
            (_, i0), (_, i1) = tops[2 * u], tops[2 * u + 1]
            best, pos = picks[u]
            a_sel, b_sel = _cand_ab(pos)
            i_sel = jnp.zeros((K, tt), jnp.int32)
            j_sel = jnp.zeros((K, tt), jnp.int32)
            for a in range(K):
                i_sel = jnp.where(a_sel == a, i0[a:a + 1, :], i_sel)
                j_sel = jnp.where(b_sel == a, i1[a:a + 1, :], j_sel)
            e = jnp.exp(best - best[0:1, :])
            i_scr[2 * hh + u] = i_sel.astype(F32)
            j_scr[2 * hh + u] = j_sel.astype(F32)
            g_scr[2 * hh + u] = e / jnp.sum(e, axis=0, keepdims=True)
        return carry

    lax.fori_loop(0, PEER_H // 2, head_pair, 0)
    nrow = PEER_H * K
    i_ref[...] = i_scr[...].reshape(nrow, tt).T.astype(jnp.int32)
    j_ref[...] = j_scr[...].reshape(nrow, tt).T.astype(jnp.int32)
    g_ref[...] = g_scr[...].reshape(nrow, tt).T


def _peer_topk(h2, wq, keys, tt=ROW_BLOCK):
    T = h2.shape[0]
    npick = PEER_H * PEER_TOPK
    ospec = pl.BlockSpec((tt, npick), lambda i: (i, 0))
    return pl.pallas_call(
        functools.partial(_peer_topk_kernel, tt=tt),
        out_shape=[jax.ShapeDtypeStruct((T, npick), jnp.int32),
                   jax.ShapeDtypeStruct((T, npick), jnp.int32),
                   jax.ShapeDtypeStruct((T, npick), F32)],
        grid=(T // tt,),
        in_specs=[pl.BlockSpec((tt, D), lambda i: (i, 0)),
                  pl.BlockSpec((D, D), lambda i: (0, 0)),
                  pl.BlockSpec((PEER_H, 2, PEER_NKEYS, PEER_DHALF), lambda i: (0, 0, 0, 0))],
        out_specs=[ospec, ospec, ospec],
        scratch_shapes=[pltpu.VMEM((2 * PEER_H, tt, PEER_DHALF), BF16),
                        pltpu.VMEM((4, PEER_NKEYS, tt), F32),
                        pltpu.VMEM((2, CAND_ROWS, tt), F32),
                        pltpu.VMEM((PEER_H, PEER_TOPK, tt), F32),
                        pltpu.VMEM((PEER_H, PEER_TOPK, tt), F32),
                        pltpu.VMEM((PEER_H, PEER_TOPK, tt), F32)],
        compiler_params=_cparams(("parallel",)),
        name="peer_topk",
    )(h2, wq, keys)


def _gelu_tanh(x):
    return 0.5 * x * (1.0 + jnp.tanh(0.7978845608028654 * (x + 0.044715 * x * x * x)))


def _peer_expert_kernel(h_ref, i_ref, j_ref, g_ref, ut_ref, v_ref, o_ref, gate_scr, *, tb, et):
    e = pl.program_id(1)
    nk = PEER_NKEYS
    npick = PEER_H * PEER_TOPK
    tiles = et // nk

    @pl.when(e == 0)
    def _():
        o_ref[...] = jnp.zeros_like(o_ref)
        sub = lax.broadcasted_iota(jnp.int32, (nk, npick), 0)
        zero = jnp.zeros((nk, npick), BF16)

        def pair(tp, carry):
            lhs, rhs = [], []
            for u in range(2):
                t = 2 * tp + u
                irow = i_ref[pl.ds(t, 1), :]
                jrow = j_ref[pl.ds(t, 1), :]
                grow = g_ref[pl.ds(t, 1), :]
                lhs.append(jnp.where(sub == irow, 1.0, 0.0).astype(BF16))
                xt = jnp.where(sub == jrow, grow, 0.0).astype(BF16)
                rhs.append(jnp.concatenate([xt, zero] if u == 0 else [zero, xt], axis=1))
            g2 = _dot_nt(jnp.concatenate(lhs, axis=1), jnp.concatenate(rhs, axis=0))
            for u in range(2):
                g = g2[:, u * nk:(u + 1) * nk].astype(BF16).astype(F32)
                hi = pltpu.bitcast(g[:G_HALF], jnp.uint32)
                lo = pltpu.bitcast(g[G_HALF:], jnp.uint32)
                r0 = pl.multiple_of((2 * tp + u) * G_PITCH, 8)
                gate_scr[pl.ds(r0, G_HALF), :] = hi | (lo >> 16)
            return carry

        lax.fori_loop(0, tb // 2, pair, 0, unroll=32)

    i0 = e * tiles
    upper = i0 < G_HALF
    r0 = jnp.where(upper, i0, i0 - G_HALF)
    rows = tb // PEER_SPLIT
    for part in range(PEER_SPLIT):
        rs = slice(part * rows, (part + 1) * rows)
        s = _dot(h_ref[rs, :], ut_ref[0])
        words = jnp.concatenate(
            [gate_scr[pl.ds(part * rows * G_PITCH + r0 + il, rows, stride=G_PITCH), :]
             for il in range(tiles)], axis=1)
        bits = jnp.where(upper, words & jnp.uint32(0xFFFF0000), words << 16)
        c = (pltpu.bitcast(bits, F32) * _gelu_tanh(s)).astype(BF16)
        o_ref[rs, :] += _dot(c, v_ref[...])


def _peer_expert(h2, ii, jj, gg, ut, v, tb=PEER_TB, et=PEER_ET):
    T = h2.shape[0]
    ne = PEER_EXPERTS // et
    assert G_HALF % (et // PEER_NKEYS) == 0
    npick = PEER_H * PEER_TOPK
    tspec = pl.BlockSpec((tb, npick), lambda i, e: (i, 0))
    xspec = pl.BlockSpec((tb, D), lambda i, e: (i, 0))
    return pl.pallas_call(
        functools.partial(_peer_expert_kernel, tb=tb, et=et),
        out_shape=jax.ShapeDtypeStruct((T, D), F32),
        grid=(T // tb, ne),
        in_specs=[xspec, tspec, tspec, tspec,
                  pl.BlockSpec((1, D, et), lambda i, e: (e, 0, 0)),
                  pl.BlockSpec((et, D), lambda i, e: (e, 0))],
        out_specs=xspec,
        scratch_shapes=[pltpu.VMEM((tb * G_PITCH, LANES), jnp.uint32)],
        compiler_params=_cparams(("parallel", "arbitrary")),
        name="peer_expert",
    )(h2, ii, jj, gg, ut, v)


def _resid_kernel(*refs, emit_h):
    if emit_h:
        x1_ref, p_ref, gf_ref, g_ref, sc_ref, sh_ref, x_ref, h_ref = refs
    else:
        x1_ref, p_ref, gf_ref, x_ref = refs
    x = x1_ref[...] + gf_ref[0] * p_ref[...]
    x_ref[...] = x
    if emit_h:
        y = x * lax.rsqrt(jnp.mean(x * x, axis=-1, keepdims=True) + EPS) * g_ref[...]
        h_ref[...] = (y * (1.0 + sc_ref[0]) + sh_ref[0]).astype(BF16)


def _resid(x1, peer, mod3, row, next_norm=None, blk0=0, nblk=None):
    nblk = x1.shape[0] // ROW_BLOCK if nblk is None else nblk
    ispec = pl.BlockSpec((ROW_BLOCK, D), lambda i: (blk0 + i, 0))
    ospec = pl.BlockSpec((ROW_BLOCK, D), lambda i: (i, 0))
    mspec = lambda chunk: pl.BlockSpec((1, 1, D), lambda i: (row(blk0 + i), 0, chunk))
    in_specs = [ispec, ispec, mspec(5)]
    args = [x1, peer, mod3]
    out_shape = [jax.ShapeDtypeStruct((nblk * ROW_BLOCK, D), F32)]
    out_specs = [ospec]
    if next_norm is not None:
        g, mod3n = next_norm
        in_specs += [pl.BlockSpec((1, D), lambda i: (0, 0)), mspec(1), mspec(0)]
        args += [g.reshape(1, D), mod3n, mod3n]
        out_shape.append(jax.ShapeDtypeStruct((nblk * ROW_BLOCK, D), BF16))
        out_specs.append(ospec)
    res = pl.pallas_call(
        functools.partial(_resid_kernel, emit_h=next_norm is not None),
        out_shape=out_shape, grid=(nblk,), in_specs=in_specs, out_specs=out_specs,
        compiler_params=_cparams(("parallel",)),
        name="resid",
    )(*args)
    return res if next_norm is not None else (res[0], None)


def _pack_w_in(w):
    o = np.cumsum([0, 256, 256, 512, 512, 32, 384, 128, 64, 512, 512, 512, 512]).tolist()
    z = lambda n: jnp.zeros((D, n), w.dtype)
    w_gla = jnp.concatenate([w[:, o[0]:o[5]], z(GLA_COLS - 1568)], axis=1)
    w_mla = jnp.concatenate([w[:, o[5]:o[8]], z(MLA_COLS - 576)], axis=1)
    w_diff = w[:, o[8]:o[11]]
    w_pool = w[:, o[11]:o[12]]
    return [a.astype(BF16) for a in (w_gla, w_mla, w_diff, w_pool)]


def _pack_w_uq(w):
    w = w.reshape(MLA_Q_RANK, MLA_H, MLA_QK)
    return jnp.concatenate([w[:, :, :MLA_NOPE].reshape(MLA_Q_RANK, -1),
                            w[:, :, MLA_NOPE:].reshape(MLA_Q_RANK, -1)], axis=1).astype(BF16)


def _pack_w_ukv(w):
    w = w.reshape(MLA_KV_RANK, MLA_H, MLA_NOPE + MLA_V)
    return jnp.concatenate([w[:, :, :MLA_NOPE].reshape(MLA_KV_RANK, -1),
                            w[:, :, MLA_NOPE:].reshape(MLA_KV_RANK, -1)], axis=1).astype(BF16)


def kernel(x_prompt, x_sample, state_gla, cache_mla, cache_diff_k, cache_diff_v, c, c_ctx,
           ada_w, ada_b, norm1_g, norm2_g, w_in, w_out, gla_gk_w, gla_gk_b, gla_onorm_g,
           mla_qnorm_g, mla_w_uq, mla_kvnorm_g, mla_w_ukv, mla_q_g, mla_k_g,
           diff_q_g, diff_k_g, diff_lambda, diff_onorm_g, pool_w, pool_scale,
           peer_wq, peer_keys, peer_u, peer_v):
    B, S, _ = x_prompt.shape
    DB, DS, _ = x_sample.shape
    L = ada_w.shape[0]
    t_ctx = B * S
    t_dec = DB * DS
    assert t_ctx % ROW_BLOCK == 0 and DS % ROW_BLOCK == 0 and t_ctx % DS == 0 and DB + 1 <= 16
    row = _mod_row_fn(t_ctx // ROW_BLOCK, DS // ROW_BLOCK)

    cond = jnp.concatenate([c_ctx[None, :], c, jnp.zeros((15 - DB, D), F32)], axis=0)
    mod = _ada_mod(cond, ada_w, ada_b)
    tables = _rope_tables(DS)
    kc_all = cache_diff_k.reshape(DB, L, -1, DIFF_H * LANES)
    vc_all = cache_diff_v.reshape(DB, L, -1, DIFF_H * LANES)

    state_gla_t = jnp.swapaxes(state_gla, -1, -2)
    st_gla = st_mla = st_dk = st_dv = None
    mods = [mod[l].reshape(16, 1, 6 * D) for l in range(L)]
    T = t_ctx + t_dec
    x, h = _normmod_part(x_prompt.reshape(t_ctx, D), norm1_g[0], mods[0], row, 0, T, None, None)
    x, h = _normmod_part(x_sample.reshape(t_dec, D), norm1_g[0], mods[0], row,
                         t_ctx // ROW_BLOCK, T, x, h)
    dec_blk0 = t_ctx // DS
    for l in range(L):
        mod3 = mods[l]
        w_gla, w_mla, w_diff, w_pool = _pack_w_in(w_in[l])
        p_gla = _matmul(h, w_gla, name="in_gla")
        p_mla = _matmul(h, w_mla, name="in_mla")
        p_diff = _matmul(h, w_diff, name="in_diff")
        p_pool = _matmul(h, w_pool, name="in_pool")

        w_uq = _pack_w_uq(mla_w_uq[l])
        w_ukv = _pack_w_ukv(mla_w_ukv[l])
        gla_w = (gla_gk_w[l], gla_gk_b[l], gla_onorm_g[l])
        mla_w = (mla_qnorm_g[l], w_uq, mla_kvnorm_g[l], w_ukv, mla_q_g[l], mla_k_g[l])
        diff_w = (diff_q_g[l], diff_k_g[l], diff_lambda[l], diff_onorm_g[l])
        pool_wl = pool_w[l].astype(BF16)
        og, st_gla = _gla(p_gla, *gla_w, None, nb=B, S=S, blk0=0, o_prev=None,
                          st_prev=st_gla, lidx=l, n_layers=L)
        og, _ = _gla(p_gla, *gla_w, state_gla_t, nb=DB, S=DS, blk0=dec_blk0, o_prev=og,
                     st_prev=None, lidx=l, n_layers=L)
        om, st_mla = _mla(p_mla, *mla_w, None, None, nb=B, S=S, blk0=0, lidx=l,
                          o_prev=None, new_prev=st_mla, n_layers=L)
        om, _ = _mla(p_mla, *mla_w, cache_mla, tables, nb=DB, S=DS, blk0=dec_blk0, lidx=l,
                     o_prev=om, new_prev=None, n_layers=L)
        od, st_dk, st_dv = _diff(p_diff, *diff_w, None, None, None, nb=B, S=S, blk0=0, lidx=l,
                                 o_prev=None, k_prev=st_dk, v_prev=st_dv, n_layers=L)
        od, _, _ = _diff(p_diff, *diff_w, kc_all, vc_all, tables, nb=DB, S=DS, blk0=dec_blk0, lidx=l,
                         o_prev=od, k_prev=None, v_prev=None, n_layers=L)
        op = _pool(p_pool, pool_wl, pool_scale[l], nb=B, S=S, blk0=0, o_prev=None)
        op = _pool(p_pool, pool_wl, pool_scale[l], nb=DB, S=DS, blk0=dec_blk0, o_prev=op)

        x1, h2 = _outproj(og, om, od, op, w_out[l].astype(BF16), x, mod3, norm2_g[l], row)
        ii, jj, gg = _peer_topk(h2, peer_wq[l].astype(BF16), peer_keys[l].astype(BF16))
        ut = peer_u[l].reshape(PEER_EXPERTS // PEER_ET, PEER_ET, D).transpose(0, 2, 1).astype(BF16)
        peer = _peer_expert(h2, ii, jj, gg, ut, peer_v[l].astype(BF16))
        if l + 1 < L:
            x, h = _resid(x1, peer, mod3, row, (norm1_g[l + 1], mods[l + 1]))

    nc = t_ctx // ROW_BLOCK
    y_prompt, _ = _resid(x1, peer, mod3, row, blk0=0, nblk=nc)
    y_sample, _ = _resid(x1, peer, mod3, row, blk0=nc, nblk=t_dec // ROW_BLOCK)
    return (y_prompt.reshape(B, S, D), y_sample.reshape(DB, DS, D), jnp.swapaxes(st_gla, -1, -2), st_mla,
            st_dk.reshape(B, L, S, DIFF_H, 2 * DIFF_DQK), st_dv.reshape(B, L, S, DIFF_H, DIFF_DV))
```

```python
import functools
import math

import jax
import jax.numpy as jnp
import numpy as np
from jax import lax
from jax.experimental import pallas as pl
from jax.experimental.pallas import tpu as pltpu

F32 = jnp.float32
BF16 = jnp.bfloat16
HI = lax.Precision.HIGHEST

D = 2048
DEPTH = 2
GRID_W = 64
GROUP_W = 512
GLA_H, GLA_DK, GLA_DV, GLA_LR, GLA_NORM, GLA_CHUNK = 4, 64, 128, 16, 16.0, 64
MLA_H, MLA_NOPE, MLA_ROPE, MLA_V, MLA_QK = 4, 128, 64, 128, 192
MLA_Q_RANK, MLA_KV_RANK = 384, 128
DIFF_H, DIFF_DV, DIFF_DQK = 4, 128, 64
POOL_WINDOWS = (2, 4, 8, 16)
POOL_CH = 128
PEER_H, PEER_NKEYS, PEER_TOPK, PEER_DHALF = 8, 128, 16, 128
PEER_EXPERTS = PEER_NKEYS * PEER_NKEYS
ROPE_BASE = 10000.0
EPS = 1e-6

LANES = 128
VMEM_LIMIT = 56 * 1024 * 1024

GLA_COLS = 1664
MLA_COLS = 640
DIFF_COLS = 1536
POOL_COLS = 512

ROW_BLOCK = 256
GLA_NSEQ = 1
PEER_TB = 512
PEER_ET = 1024
PEER_SPLIT = 1
G_HALF = 64
G_PITCH = 72


def _cparams(sem):
    return pltpu.CompilerParams(dimension_semantics=sem, vmem_limit_bytes=VMEM_LIMIT)


def _dot(a, b, precision=None):
    return jnp.dot(a, b, preferred_element_type=F32, precision=precision)


def _dot_nt(a, b, precision=None):
    return lax.dot_general(a, b, (((1,), (1,)), ((), ())), preferred_element_type=F32,
                           precision=precision)


def _dot_tn(a, b, precision=None):
    return lax.dot_general(a, b, (((0,), (0,)), ((), ())), preferred_element_type=F32,
                           precision=precision)


def _silu(x):
    return x / (1.0 + jnp.exp(-x))


def _ada_kernel(c_ref, w_ref, b_ref, o_ref):
    a = _silu(c_ref[...]).astype(BF16)
    o_ref[0] = _dot(a, w_ref[0].astype(BF16)) + b_ref[0]


def _ada_mod(cond16, ada_w, ada_b):
    L, _, N = ada_w.shape
    tn = 1024
    return pl.pallas_call(
        _ada_kernel,
        out_shape=jax.ShapeDtypeStruct((L, 16, N), F32),
        grid=(L, N // tn),
        in_specs=[pl.BlockSpec((16, D), lambda l, j: (0, 0)),
                  pl.BlockSpec((1, D, tn), lambda l, j: (l, 0, j)),
                  pl.BlockSpec((1, 1, tn), lambda l, j: (l, 0, j))],
        out_specs=pl.BlockSpec((1, 16, tn), lambda l, j: (l, 0, j)),
        compiler_params=_cparams(("arbitrary", "arbitrary")),
        name="ada_mod",
    )(cond16, ada_w, ada_b.reshape(L, 1, N))


def _mod_row_fn(n_ctx_blocks, blocks_per_dec):
    def row(i):
        return jnp.where(i < n_ctx_blocks, 0, 1 + (i - n_ctx_blocks) // blocks_per_dec)
    return row


def _mod_spec(row, chunk):
    return pl.BlockSpec((1, 1, D), lambda i: (row(i), 0, chunk))


def _normmod_kernel(xa_ref, xb_ref, g_ref, sc_ref, sh_ref, xo_ref, o_ref, *, na):
    def emit(x_ref):
        x = x_ref[...]
        xo_ref[...] = x
        y = x * lax.rsqrt(jnp.mean(x * x, axis=-1, keepdims=True) + EPS) * g_ref[...]
        o_ref[...] = (y * (1.0 + sc_ref[0]) + sh_ref[0]).astype(BF16)

    @pl.when(pl.program_id(0) < na)
    def _():
        emit(xa_ref)

    @pl.when(pl.program_id(0) >= na)
    def _():
        emit(xb_ref)


def _normmod_join(xa, xb, g, mod3, row):
    na, nb = xa.shape[0] // ROW_BLOCK, xb.shape[0] // ROW_BLOCK
    T = xa.shape[0] + xb.shape[0]
    ospec = pl.BlockSpec((ROW_BLOCK, D), lambda i: (i, 0))
    return pl.pallas_call(
        functools.partial(_normmod_kernel, na=na),
        out_shape=[jax.ShapeDtypeStruct((T, D), F32), jax.ShapeDtypeStruct((T, D), BF16)],
        grid=(na + nb,),
        in_specs=[pl.BlockSpec((ROW_BLOCK, D), lambda i: (jnp.minimum(i, na - 1), 0)),
                  pl.BlockSpec((ROW_BLOCK, D), lambda i: (jnp.maximum(i - na, 0), 0)),
                  pl.BlockSpec((1, D), lambda i: (0, 0)), _mod_spec(row, 1), _mod_spec(row, 0)],
        out_specs=[ospec, ospec],
        compiler_params=_cparams(("parallel",)),
        name="normmod",
    )(xa, xb, g.reshape(1, D), mod3, mod3)


def _mm_kernel(h_ref, w_ref, o_ref):
    o_ref[...] = _dot(h_ref[...], w_ref[...]).astype(o_ref.dtype)


def _matmul(h, w, tm=512, out_dtype=F32, name="matmul"):
    T, K = h.shape
    N = w.shape[1]
    return pl.pallas_call(
        _mm_kernel,
        out_shape=jax.ShapeDtypeStruct((T, N), out_dtype),
        grid=(T // tm,),
        in_specs=[pl.BlockSpec((tm, K), lambda i: (i, 0)),
                  pl.BlockSpec((K, N), lambda i: (0, 0))],
        out_specs=pl.BlockSpec((tm, N), lambda i: (i, 0)),
        compiler_params=_cparams(("parallel",)),
        name=name,
    )(h, w)


def _gla_kernel(*refs, S, has_state, emit_state, unroll):
    if has_state:
        p_ref, gkw_ref, gkb_ref, on_ref, s0_ref = refs[:5]
        rest = refs[5:]
    else:
        p_ref, gkw_ref, gkb_ref, on_ref = refs[:4]
        s0_ref = None
        rest = refs[4:]
    if emit_state:
        o_ref, st_ref, lg_scr, of_scr, st_scr = rest
    else:
        o_ref, lg_scr, of_scr, st_scr = rest
        st_ref = None
    C = GLA_CHUNK
    n = S // C
    H, DK, DV = GLA_H, GLA_DK, GLA_DV
    HK = H * DK
    r_i = lax.broadcasted_iota(jnp.int32, (C, C), 0)
    c_i = lax.broadcasted_iota(jnp.int32, (C, C), 1)
    on_g = on_ref[...]

    for d in (0, 1):
        lr = p_ref[:, 1536 + GLA_LR * d:1536 + GLA_LR * (d + 1)]
        x = _dot(lr, gkw_ref[d], HI) + gkb_ref[d]
        lg_scr[...] = (jnp.minimum(x, 0.0) - jnp.log(1.0 + jnp.exp(-jnp.abs(x)))) * (1.0 / GLA_NORM)
        if d == 0:
            causal = r_i >= c_i
        else:
            causal = r_i <= c_i
        cmat = jnp.where(causal, 1.0, 0.0).astype(BF16)
        for s in range(GLA_NSEQ):
            for h in range(H):
                if has_state:
                    st_scr[s * H + h] = s0_ref[s, 0, d, h]
                else:
                    st_scr[s * H + h] = jnp.zeros((DV, DK), F32)

        def chunk(ci, carry, d=d, cmat=cmat, causal=causal):
            c = ci if d == 0 else n - 1 - ci
            for s in range(GLA_NSEQ):
                r0 = pl.multiple_of(s * S + c * C, C)
                gc = lg_scr[pl.ds(r0, C), :]
                g1 = gc.astype(BF16)
                rem = gc - g1.astype(F32)
                g2 = rem.astype(BF16)
                g3 = (rem - g2.astype(F32)).astype(BF16)
                cs = _dot(cmat, jnp.concatenate([g1, g2, g3], axis=1))
                cum = cs[:, :HK] + cs[:, HK:2 * HK] + cs[:, 2 * HK:]
                q = p_ref[pl.ds(r0, C), 0:256] * (DK ** -0.5)
                k = p_ref[pl.ds(r0, C), 256:512]
                qe = (q * jnp.exp(cum)).astype(BF16)
                ke = (k * jnp.exp(-cum)).astype(BF16)
                tot = jnp.sum(gc, axis=0, keepdims=True)
                kd = (k * jnp.exp(tot - cum)).astype(BF16)
                dec = jnp.exp(tot)
                outs = []
                for h in range(H):
                    sl = slice(h * DK, (h + 1) * DK)
                    v_h = p_ref[pl.ds(r0, C), 512 + h * DV:512 + (h + 1) * DV].astype(BF16)
                    st = st_scr[s * H + h]
                    att = _dot_nt(qe[:, sl], ke[:, sl])
                    att = jnp.where(causal, att, 0.0).astype(BF16)
                    o_h = _dot_nt(qe[:, sl], st.astype(BF16)) + _dot(att, v_h)
                    st_scr[s * H + h] = dec[:, sl] * st + _dot_tn(v_h, kd[:, sl])
                    outs.append(o_h)
                o_c = jnp.concatenate(outs, axis=1)
                if d == 0:
                    of_scr[pl.ds(r0, C), :] = o_c
                else:
                    o_c = o_c + of_scr[pl.ds(r0, C), :]
                    gate = p_ref[pl.ds(r0, C), 1024:1536]
                    res = []
                    for h in range(H):
                        oh = o_c[:, h * DV:(h + 1) * DV]
                        oh = oh * lax.rsqrt(jnp.mean(oh * oh, axis=-1, keepdims=True) + EPS) * on_g
                        res.append(oh * _silu(gate[:, h * DV:(h + 1) * DV]))
                    o_ref[pl.ds(r0, C), :] = jnp.concatenate(res, axis=1).astype(o_ref.dtype)
            return carry

        lax.fori_loop(0, n, chunk, 0, unroll=unroll)
        if emit_state:
            for s in range(GLA_NSEQ):
                for h in range(H):
                    st_ref[s, 0, d, h] = st_scr[s * H + h]


def _carried_call(kern, *, grid, in_specs, args, outs, scratch_shapes, name):
    n_in = len(args)
    carried = [(k, o[2]) for k, o in enumerate(outs) if o[2] is not None]
    n_carried = len(carried)

    def body(*refs):
        kern(*refs[:n_in], *refs[n_in + n_carried:])

    return pl.pallas_call(
        body,
        out_shape=[o[0] for o in outs], grid=grid,
        in_specs=list(in_specs) + [pl.BlockSpec(memory_space=pl.ANY)] * n_carried,
        out_specs=[o[1] for o in outs],
        scratch_shapes=scratch_shapes,
        input_output_aliases={n_in + j: k for j, (k, _) in enumerate(carried)},
        compiler_params=_cparams(("parallel",)),
        name=name,
    )(*args, *[prev for _, prev in carried])


def _gla(proj, gk_w, gk_b, onorm_g, s0, *, nb, S, blk0, o_prev, st_prev, lidx, n_layers):
    has_state = s0 is not None
    emit_state = not has_state
    ns = GLA_NSEQ
    assert nb % ns == 0 and blk0 % ns == 0
    gb0 = blk0 // ns
    rows = ns * S
    in_specs = [pl.BlockSpec((rows, GLA_COLS), lambda b: (gb0 + b, 0)),
                pl.BlockSpec((2, GLA_LR, GLA_H * GLA_DK), lambda b: (0, 0, 0)),
                pl.BlockSpec((2, 1, GLA_H * GLA_DK), lambda b: (0, 0, 0)),
                pl.BlockSpec((1, GLA_DV), lambda b: (0, 0))]
    args = [proj, gk_w, gk_b.reshape(2, 1, -1), onorm_g.reshape(1, -1)]
    st_dims = (2, GLA_H, GLA_DV, GLA_DK)
    if has_state:
        in_specs.append(pl.BlockSpec((ns, 1) + st_dims, lambda b: (b, lidx, 0, 0, 0, 0)))
        args.append(s0)
    outs = [(jax.ShapeDtypeStruct((proj.shape[0], GLA_H * GLA_DV), BF16),
             pl.BlockSpec((rows, GLA_H * GLA_DV), lambda b: (gb0 + b, 0)), o_prev)]
    if emit_state:
        outs.append((jax.ShapeDtypeStruct((nb, n_layers) + st_dims, F32),
                     pl.BlockSpec((ns, 1) + st_dims, lambda b: (b, lidx, 0, 0, 0, 0)), st_prev))
    res = _carried_call(
        functools.partial(_gla_kernel, S=S, has_state=has_state, emit_state=emit_state,
                          unroll=4 // ns),
        grid=(nb // ns,), in_specs=in_specs, args=args, outs=outs,
        scratch_shapes=[pltpu.VMEM((rows, GLA_H * GLA_DK), F32),
                        pltpu.VMEM((rows, GLA_H * GLA_DV), F32),
                        pltpu.VMEM((ns * GLA_H, GLA_DV, GLA_DK), F32)],
        name="gla_dec" if has_state else "gla_ctx")
    return res if emit_state else (res[0], None)


def _rope_tables(n_tokens):
    t = jnp.arange(n_tokens)
    row = (t // GRID_W).astype(F32)
    col = (t % GRID_W).astype(F32)
    half = 16
    freq = ROPE_BASE ** (-jnp.arange(half, dtype=F32) / half)
    ar = row[:, None] * freq[None, :]
    ac = col[:, None] * freq[None, :]
    cos = jnp.concatenate([jnp.cos(ar), jnp.cos(ar), jnp.cos(ac), jnp.cos(ac)], axis=1)
    sin = jnp.concatenate([-jnp.sin(ar), jnp.sin(ar), -jnp.sin(ac), jnp.sin(ac)], axis=1)
    return jnp.tile(cos, (1, 2)), jnp.tile(sin, (1, 2))


def _rope128(x, cos, sin):
    lane = lax.broadcasted_iota(jnp.int32, x.shape, 1)
    first = (lane % 32) < 16
    partner = jnp.where(first, pltpu.roll(x, LANES - 16, 1), pltpu.roll(x, 16, 1))
    return x * cos + partner * sin


def _mla_kernel(*refs, S, P, rope, qb):
    if P:
        (p_ref, qn_ref, wuq_ref, kvn_ref, wukv_ref, qg_ref, kg_ref, cache_ref, cos_ref, sin_ref,
         o_ref, k_scr, v_scr) = refs
        new_ref = None
    else:
        (p_ref, qn_ref, wuq_ref, kvn_ref, wukv_ref, qg_ref, kg_ref,
         o_ref, new_ref, k_scr, v_scr) = refs
    H = MLA_H
    NK = P + S
    qg = qg_ref[...]
    kg = kg_ref[...]
    qg_n, qg_r = qg[:, :MLA_NOPE], qg[:, MLA_NOPE:]
    kg_n, kg_r = kg[:, :MLA_NOPE], kg[:, MLA_NOPE:]
    qg_r2 = jnp.concatenate([qg_r, qg_r], axis=1)
    kg_r2 = jnp.concatenate([kg_r, kg_r], axis=1)

    ckv = p_ref[:, 384:512]
    ckv = ckv * lax.rsqrt(jnp.mean(ckv * ckv, axis=-1, keepdims=True) + EPS) * kvn_ref[...]
    kpe = p_ref[:, 512:576]
    if new_ref is not None:
        new_ref[0, 0, :, 0:MLA_KV_RANK] = ckv
        new_ref[0, 0, :, MLA_KV_RANK:] = kpe

    def expand(c, pe, do_rope, r0, nrows):
        kv = _dot(c.astype(BF16), wukv_ref[...])
        pe2 = jnp.concatenate([pe, pe], axis=1)
        pe_ss = jnp.sum(pe * pe, axis=-1, keepdims=True)
        pe_g = pe2 * kg_r2
        if do_rope:
            pe_g = _rope128(pe_g, cos_ref[...], sin_ref[...])
        for h in range(H):
            kn = kv[:, h * MLA_NOPE:(h + 1) * MLA_NOPE]
            r = lax.rsqrt((jnp.sum(kn * kn, axis=-1, keepdims=True) + pe_ss) / MLA_QK + EPS)
            kfull = jnp.concatenate([kn * kg_n * r, pe_g[:, :MLA_ROPE] * r], axis=1)
            k_scr[h, r0:r0 + nrows, :] = kfull.astype(BF16)
            v_scr[h, r0:r0 + nrows, :] = kv[:, 512 + h * MLA_V:512 + (h + 1) * MLA_V].astype(BF16)

    if P:
        cc = cache_ref[0, 0]
        expand(cc[:, :MLA_KV_RANK], cc[:, MLA_KV_RANK:], False, 0, P)
    expand(ckv, kpe, rope, P, S)

    cq = p_ref[:, 0:384]
    cq = cq * lax.rsqrt(jnp.mean(cq * cq, axis=-1, keepdims=True) + EPS) * qn_ref[...]
    qall = _dot(cq.astype(BF16), wuq_ref[...])
    scale = MLA_QK ** -0.5
    lane = lax.broadcasted_iota(jnp.int32, (S, LANES), 1)
    for hp in range(H // 2):
        rs = []
        qns = []
        for j in range(2):
            h = 2 * hp + j
            qn = qall[:, h * MLA_NOPE:(h + 1) * MLA_NOPE]
            qr = qall[:, 512 + h * MLA_ROPE:512 + (h + 1) * MLA_ROPE]
            ss = jnp.sum(qn * qn, axis=-1, keepdims=True) + jnp.sum(qr * qr, axis=-1, keepdims=True)
            r = lax.rsqrt(ss / MLA_QK + EPS)
            rs.append(r)
            qns.append(qn * qg_n * r)
        qr2 = qall[:, 512 + hp * LANES:512 + (hp + 1) * LANES] * qg_r2
        qr2 = qr2 * jnp.where(lane < MLA_ROPE, rs[0], rs[1])
        if rope:
            qr2 = _rope128(qr2, cos_ref[...], sin_ref[...])
        for j in range(2):
            h = 2 * hp + j
            qfull = jnp.concatenate([qns[j], qr2[:, j * MLA_ROPE:(j + 1) * MLA_ROPE]], axis=1)
            qfull = (qfull * scale).astype(BF16)
            kh = k_scr[h]
            vh = v_scr[h]
            for i in range(S // qb):
                sc = _dot_nt(qfull[i * qb:(i + 1) * qb], kh)
                m = jnp.max(sc, axis=-1, keepdims=True)
                e = jnp.exp(sc - m)
                o = _dot(e.astype(BF16), vh) / jnp.sum(e, axis=-1, keepdims=True)
                o_ref[i * qb:(i + 1) * qb, h * MLA_V:(h + 1) * MLA_V] = o.astype(o_ref.dtype)


def _mla(proj, qnorm_g, w_uq, kvnorm_g, w_ukv, q_g, k_g, cache, tables, *, nb, S, blk0, lidx,
         o_prev, new_prev, n_layers):
    P = 0 if cache is None else cache.shape[2]
    const2 = lambda b: (0, 0)
    in_specs = [pl.BlockSpec((S, MLA_COLS), lambda b: (blk0 + b, 0)),
                pl.BlockSpec((1, MLA_Q_RANK), const2),
                pl.BlockSpec((MLA_Q_RANK, MLA_H * MLA_QK), const2),
                pl.BlockSpec((1, MLA_KV_RANK), const2),
                pl.BlockSpec((MLA_KV_RANK, MLA_H * (MLA_NOPE + MLA_V)), const2),
                pl.BlockSpec((1, MLA_QK), const2),
                pl.BlockSpec((1, MLA_QK), const2)]
    args = [proj, qnorm_g.reshape(1, -1), w_uq, kvnorm_g.reshape(1, -1), w_ukv,
            q_g.reshape(1, -1), k_g.reshape(1, -1)]
    if P:
        in_specs += [pl.BlockSpec((1, 1, P, MLA_KV_RANK + MLA_ROPE), lambda b: (b, lidx, 0, 0)),
                     pl.BlockSpec((S, LANES), const2), pl.BlockSpec((S, LANES), const2)]
        args += [cache, tables[0], tables[1]]
    outs = [(jax.ShapeDtypeStruct((proj.shape[0], MLA_H * MLA_V), BF16),
             pl.BlockSpec((S, MLA_H * MLA_V), lambda b: (blk0 + b, 0)), o_prev)]
    if not P:
        cw = MLA_KV_RANK + MLA_ROPE
        outs.append((jax.ShapeDtypeStruct((nb, n_layers, S, cw), F32),
                     pl.BlockSpec((1, 1, S, cw), lambda b: (b, lidx, 0, 0)), new_prev))
    res = _carried_call(
        functools.partial(_mla_kernel, S=S, P=P, rope=bool(P), qb=min(S, 256)),
        grid=(nb,), in_specs=in_specs, args=args, outs=outs,
        scratch_shapes=[pltpu.VMEM((MLA_H, P + S, MLA_QK), BF16),
                        pltpu.VMEM((MLA_H, P + S, MLA_V), BF16)],
        name="mla_dec" if P else "mla_ctx")
    return (res[0], None) if P else res


def _group_rms64(x, g2):
    lane = lax.broadcasted_iota(jnp.int32, x.shape, 1)
    lo = lane < DIFF_DQK
    xx = x * x
    s_all = jnp.sum(xx, axis=-1, keepdims=True)
    s_lo = jnp.sum(jnp.where(lo, xx, 0.0), axis=-1, keepdims=True)
    ms = jnp.where(lo, s_lo, s_all - s_lo) * (1.0 / DIFF_DQK)
    return x * lax.rsqrt(ms + EPS) * g2


def _diff_kernel(*refs, S, P, rope, qb, lam_init):
    if P:
        (p_ref, qg_ref, kg_ref, lam_ref, on_ref, kc_ref, vc_ref, cos_ref, sin_ref,
         o_ref, k_scr, v_scr) = refs
        ko_ref = vo_ref = None
    else:
        (p_ref, qg_ref, kg_ref, lam_ref, on_ref, o_ref, ko_ref, vo_ref, k_scr, v_scr) = refs
    H = DIFF_H
    qg2 = jnp.concatenate([qg_ref[...], qg_ref[...]], axis=1)
    kg2 = jnp.concatenate([kg_ref[...], kg_ref[...]], axis=1)
    lp = lam_ref[...]
    l1 = jnp.sum(jnp.sum(lp[0:1] * lp[1:2], axis=-1, keepdims=True), axis=0, keepdims=True)
    l2 = jnp.sum(jnp.sum(lp[2:3] * lp[3:4], axis=-1, keepdims=True), axis=0, keepdims=True)
    lam = jnp.exp(l1) - jnp.exp(l2) + lam_init
    scale = DIFF_DQK ** -0.5
    lane = lax.broadcasted_iota(jnp.int32, (S, LANES), 1)
    lo = lane < DIFF_DQK
    on_g = on_ref[...]

    if vo_ref is not None:
        vo_ref[0, 0] = p_ref[:, 1024:1536]
    for h in range(H):
        sl = slice(h * LANES, (h + 1) * LANES)
        kn = _group_rms64(p_ref[:, 512 + h * LANES:512 + (h + 1) * LANES], kg2)
        if ko_ref is not None:
            ko_ref[0, 0, :, sl] = kn
        if rope:
            kn = _rope128(kn, cos_ref[...], sin_ref[...])
        if P:
            k_scr[0:P, :] = kc_ref[0, 0, :, sl].astype(BF16)
            v_scr[0:P, :] = vc_ref[0, 0, :, sl].astype(BF16)
        k_scr[P:P + S, :] = kn.astype(BF16)
        v_scr[P:P + S, :] = p_ref[:, 1024 + h * LANES:1024 + (h + 1) * LANES].astype(BF16)
        qn = _group_rms64(p_ref[:, sl], qg2)
        if rope:
            qn = _rope128(qn, cos_ref[...], sin_ref[...])
        qn = qn * scale
        q0 = jnp.where(lo, qn, 0.0).astype(BF16)
        q1 = jnp.where(lo, 0.0, qn).astype(BF16)
        kh = k_scr[...]
        vh = v_scr[...]
        for i in range(S // qb):
            rows = slice(i * qb, (i + 1) * qb)
            ps = []
            for qq in (q0, q1):
                sc = _dot_nt(qq[rows], kh)
                m = jnp.max(sc, axis=-1, keepdims=True)
                e = jnp.exp(sc - m)
                ps.append(e / jnp.sum(e, axis=-1, keepdims=True))
            a = (ps[0] - lam * ps[1]).astype(BF16)
            o = _dot(a, vh)
            o = o * lax.rsqrt(jnp.mean(o * o, axis=-1, keepdims=True) + EPS) * on_g
            o_ref[rows, sl] = (o * (1.0 - lam_init)).astype(o_ref.dtype)


def _diff(proj, q_g, k_g, lam_p, onorm_g, kc, vc, tables, *, nb, S, blk0, lidx,
          o_prev, k_prev, v_prev, n_layers):
    P = 0 if kc is None else kc.shape[2]
    const2 = lambda b: (0, 0)
    in_specs = [pl.BlockSpec((S, DIFF_COLS), lambda b: (blk0 + b, 0)),
                pl.BlockSpec((1, DIFF_DQK), const2), pl.BlockSpec((1, DIFF_DQK), const2),
                pl.BlockSpec((4, DIFF_DQK), const2), pl.BlockSpec((1, DIFF_DV), const2)]
    args = [proj, q_g.reshape(1, -1), k_g.reshape(1, -1), lam_p, onorm_g.reshape(1, -1)]
    if P:
        cspec = pl.BlockSpec((1, 1, P, DIFF_H * LANES), lambda b: (b, lidx, 0, 0))
        in_specs += [cspec, cspec, pl.BlockSpec((S, LANES), const2), pl.BlockSpec((S, LANES), const2)]
        args += [kc, vc, tables[0], tables[1]]
    lam_init = 0.8 - 0.6 * math.exp(-0.3 * lidx)
    outs = [(jax.ShapeDtypeStruct((proj.shape[0], DIFF_H * DIFF_DV), BF16),
             pl.BlockSpec((S, DIFF_H * DIFF_DV), lambda b: (blk0 + b, 0)), o_prev)]
    if not P:
        cshape = jax.ShapeDtypeStruct((nb, n_layers, S, DIFF_H * LANES), F32)
        cspec = pl.BlockSpec((1, 1, S, DIFF_H * LANES), lambda b: (b, lidx, 0, 0))
        outs += [(cshape, cspec, k_prev), (cshape, cspec, v_prev)]
    res = _carried_call(
        functools.partial(_diff_kernel, S=S, P=P, rope=bool(P), qb=min(S, 256), lam_init=lam_init),
        grid=(nb,), in_specs=in_specs, args=args, outs=outs,
        scratch_shapes=[pltpu.VMEM((P + S, LANES), BF16), pltpu.VMEM((P + S, LANES), BF16)],
        name="diff_dec" if P else "diff_ctx")
    return (res[0], None, None) if P else res


POOL_PAD = 8


def _pool_kernel(p_ref, w_ref, sc_ref, o_ref, pad_scr, *, S):
    zeros = jnp.zeros((POOL_PAD, GROUP_W), F32)
    pad_scr[0:POOL_PAD, :] = zeros
    pad_scr[POOL_PAD + S:POOL_PAD + S + POOL_PAD, :] = zeros
    pad_scr[POOL_PAD:POOL_PAD + S, :] = p_ref[...]
    t = lax.broadcasted_iota(jnp.int32, (S, POOL_CH), 0)
    for gi, win in enumerate(POOL_WINDOWS):
        cs = slice(gi * POOL_CH, (gi + 1) * POOL_CH)
        acc = None
        for dlt in range(-(win // 2), win // 2):
            piece = pad_scr[POOL_PAD + dlt:POOL_PAD + dlt + S, cs]
            acc = piece if acc is None else acc + piece
        cnt = (jnp.minimum(t + win // 2, S) - jnp.maximum(t - win // 2, 0)).astype(F32)
        y = (acc / cnt - p_ref[:, cs]).astype(BF16)
        o_ref[:, cs] = (_dot(y, w_ref[gi]) * sc_ref[:, cs]).astype(o_ref.dtype)


def _pool(proj, w, scale, *, nb, S, blk0, o_prev):
    return _carried_call(
        functools.partial(_pool_kernel, S=S),
        grid=(nb,),
        in_specs=[pl.BlockSpec((S, POOL_COLS), lambda b: (blk0 + b, 0)),
                  pl.BlockSpec((len(POOL_WINDOWS), POOL_CH, POOL_CH), lambda b: (0, 0, 0)),
                  pl.BlockSpec((1, GROUP_W), lambda b: (0, 0))],
        args=[proj, w, scale.reshape(1, -1)],
        outs=[(jax.ShapeDtypeStruct((proj.shape[0], GROUP_W), BF16),
               pl.BlockSpec((S, GROUP_W), lambda b: (blk0 + b, 0)), o_prev)],
        scratch_shapes=[pltpu.VMEM((S + 2 * POOL_PAD, GROUP_W), F32)],
        name="pool")[0]


def _outproj_kernel(og_ref, om_ref, od_ref, op_ref, w_ref, x_ref, ga_ref, g2_ref, sc_ref, sh_ref,
                    x1_ref, h2_ref):
    mix = _dot(og_ref[...], w_ref[0:512, :])
    mix += _dot(om_ref[...], w_ref[512:1024, :])
    mix += _dot(od_ref[...], w_ref[1024:1536, :])
    mix += _dot(op_ref[...], w_ref[1536:2048, :])
    x1 = x_ref[...] + ga_ref[0] * mix
    x1_ref[...] = x1
    y = x1 * lax.rsqrt(jnp.mean(x1 * x1, axis=-1, keepdims=True) + EPS) * g2_ref[...]
    h2_ref[...] = (y * (1.0 + sc_ref[0]) + sh_ref[0]).astype(BF16)


def _outproj(og, om, od, op, w_out, x, mod3, norm2_g, row):
    T = x.shape[0]
    ospec = pl.BlockSpec((ROW_BLOCK, GROUP_W), lambda i: (i, 0))
    xspec = pl.BlockSpec((ROW_BLOCK, D), lambda i: (i, 0))
    return pl.pallas_call(
        _outproj_kernel,
        out_shape=[jax.ShapeDtypeStruct((T, D), F32), jax.ShapeDtypeStruct((T, D), BF16)],
        grid=(T // ROW_BLOCK,),
        in_specs=[ospec, ospec, ospec, ospec,
                  pl.BlockSpec((D, D), lambda i: (0, 0)),
                  xspec, _mod_spec(row, 2),
                  pl.BlockSpec((1, D), lambda i: (0, 0)),
                  _mod_spec(row, 4), _mod_spec(row, 3)],
        out_specs=[xspec, xspec],
        compiler_params=_cparams(("parallel",)),
        name="outproj",
    )(og, om, od, op, w_out, x, mod3, norm2_g.reshape(1, D), mod3, mod3)


def _topk_chains(x_refs, nrows, tt):
    rows = lax.broadcasted_iota(jnp.int32, (nrows, tt), 0)
    krow = lax.broadcasted_iota(jnp.int32, (PEER_TOPK, tt), 0)
    ng = nrows // 8
    rows_f = rows.astype(F32)
    sub = lax.broadcasted_iota(jnp.int32, (8, tt), 0).astype(F32)
    row_groups = [sub + float(8 * g) for g in range(ng)]

    def argmax_rows(x):
        items = [(x[8 * g:8 * (g + 1)], row_groups[g]) for g in range(ng)]
        while len(items) > 1:
            nxt = []
            for j in range(0, len(items) - 1, 2):
                (a, ra), (b, rb) = items[j], items[j + 1]
                gt = b > a
                nxt.append((jnp.where(gt, b, a), jnp.where(gt, rb, ra)))
            if len(items) % 2:
                nxt.append(items[-1])
            items = nxt
        v, r = items[0]
        m = jnp.max(v, axis=0, keepdims=True)
        pos = jnp.min(jnp.where(v == m, r, float(nrows)), axis=0, keepdims=True)
        return m, pos

    def body(k, carry):
        out = []
        for x_ref, (vals, sel) in zip(x_refs, carry):
            x = x_ref[...]
            m, pos = argmax_rows(x)
            x_ref[...] = jnp.where(rows_f == pos, -jnp.inf, x)
            out.append((jnp.where(krow == k, m, vals), jnp.where(krow == k, pos, sel)))
        return tuple(out)

    init = (jnp.zeros((PEER_TOPK, tt), F32), jnp.zeros((PEER_TOPK, tt), F32))
    res = lax.fori_loop(0, PEER_TOPK, body, tuple(init for _ in x_refs))
    return [(vals, sel.astype(jnp.int32)) for vals, sel in res]


CAND_ROWS = 80


def _cand_ab(r):
    a = jnp.where(r < 16, 0, jnp.where(r < 72, 1 + ((r - 16) >> 3), r - 64))
    b = jnp.where(r < 16, r, jnp.where(r < 72, (r - 16) & 7, 0))
    return a, b


def _peer_topk_kernel(h_ref, wq_ref, keys_ref, i_ref, j_ref, g_ref,
                      q_scr, sc_scr, cand_scr, i_scr, j_scr, g_scr, *, tt):
    q = _dot(h_ref[...], wq_ref[...]).astype(BF16)
    for hp in range(2 * PEER_H):
        q_scr[hp] = q[:, hp * PEER_DHALF:(hp + 1) * PEER_DHALF]
    K = PEER_TOPK

    def head_pair(hh, carry):
        for u in range(2):
            for p in range(2):
                h = 2 * hh + u
                sc_scr[2 * u + p] = _dot_nt(keys_ref[h, p], q_scr[2 * h + p])
        tops = _topk_chains([sc_scr.at[c] for c in range(4)], PEER_NKEYS, tt)
        for u in range(2):
            (s0, _), (s1, _) = tops[2 * u], tops[2 * u + 1]
            cand_scr[u, 0:K, :] = s0[0:1, :] + s1
            for a in range(1, 8):
                cand_scr[u, 8 + 8 * a:16 + 8 * a, :] = s0[a:a + 1, :] + s1[0:8, :]
            cand_scr[u, 72:80, :] = s0[8:16, :] + s1[0:1, :]
        picks = _topk_chains([cand_scr.at[u] for u in range(2)], CAND_ROWS, tt)
        for u in range(2):
            (_, i0), (_, i1) = tops[2 * u], tops[2 * u + 1]
            best, pos = picks[u]
            a_sel, b_sel = _cand_ab(pos)
            i_sel = jnp.zeros((K, tt), jnp.int32)
            j_sel = jnp.zeros((K, tt), jnp.int32)
            for a in range(K):
                i_sel = jnp.where(a_sel == a, i0[a:a + 1, :], i_sel)
                j_sel = jnp.where(b_sel == a, i1[a:a + 1, :], j_sel)
            e = jnp.exp(best - best[0:1, :])
            i_scr[2 * hh + u] = i_sel.astype(F32)
            j_scr[2 * hh + u] = j_sel.astype(F32)
            g_scr[2 * hh + u] = e / jnp.sum(e, axis=0, keepdims=True)
        return carry

    lax.fori_loop(0, PEER_H // 2, head_pair, 0)
    nrow = PEER_H * K
    i_ref[...] = i_scr[...].reshape(nrow, tt).T.astype(jnp.int32)
    j_ref[...] = j_scr[...].reshape(nrow, tt).T.astype(jnp.int32)
    g_ref[...] = g_scr[...].reshape(nrow, tt).T


def _peer_topk(h2, wq, keys, tt=ROW_BLOCK):
    T = h2.shape[0]
    npick = PEER_H * PEER_TOPK
    ospec = pl.BlockSpec((tt, npick), lambda i: (i, 0))
    return pl.pallas_call(
        functools.partial(_peer_topk_kernel, tt=tt),
        out_shape=[jax.ShapeDtypeStruct((T, npick), jnp.int32),
                   jax.ShapeDtypeStruct((T, npick), jnp.int32),
                   jax.ShapeDtypeStruct((T, npick), F32)],
        grid=(T // tt,),
        in_specs=[pl.BlockSpec((tt, D), lambda i: (i, 0)),
                  pl.BlockSpec((D, D), lambda i: (0, 0)),
                  pl.BlockSpec((PEER_H, 2, PEER_NKEYS, PEER_DHALF), lambda i: (0, 0, 0, 0))],
        out_specs=[ospec, ospec, ospec],
        scratch_shapes=[pltpu.VMEM((2 * PEER_H, tt, PEER_DHALF), BF16),
                        pltpu.VMEM((4, PEER_NKEYS, tt), F32),
                        pltpu.VMEM((2, CAND_ROWS, tt), F32),
                        pltpu.VMEM((PEER_H, PEER_TOPK, tt), F32),
                        pltpu.VMEM((PEER_H, PEER_TOPK, tt), F32),
                        pltpu.VMEM((PEER_H, PEER_TOPK, tt), F32)],
        compiler_params=_cparams(("parallel",)),
        name="peer_topk",
    )(h2, wq, keys)


def _gelu_tanh(x):
    return 0.5 * x * (1.0 + jnp.tanh(0.7978845608028654 * (x + 0.044715 * x * x * x)))


def _peer_expert_kernel(h_ref, i_ref, j_ref, g_ref, ut_ref, v_ref, o_ref, gate_scr, *, tb, et):
    e = pl.program_id(1)
    nk = PEER_NKEYS
    npick = PEER_H * PEER_TOPK
    tiles = et // nk

    @pl.when(e == 0)
    def _():
        o_ref[...] = jnp.zeros_like(o_ref)
        sub = lax.broadcasted_iota(jnp.int32, (nk, npick), 0)
        zero = jnp.zeros((nk, npick), BF16)

        def pair(tp, carry):
            lhs, rhs = [], []
            for u in range(2):
                t = 2 * tp + u
                irow = i_ref[pl.ds(t, 1), :]
                jrow = j_ref[pl.ds(t, 1), :]
                grow = g_ref[pl.ds(t, 1), :]
                lhs.append(jnp.where(sub == irow, 1.0, 0.0).astype(BF16))
                xt = jnp.where(sub == jrow, grow, 0.0).astype(BF16)
                rhs.append(jnp.concatenate([xt, zero] if u == 0 else [zero, xt], axis=1))
            g2 = _dot_nt(jnp.concatenate(lhs, axis=1), jnp.concatenate(rhs, axis=0))
            for u in range(2):
                g = g2[:, u * nk:(u + 1) * nk].astype(BF16).astype(F32)
                hi = pltpu.bitcast(g[:G_HALF], jnp.uint32)
                lo = pltpu.bitcast(g[G_HALF:], jnp.uint32)
                r0 = pl.multiple_of((2 * tp + u) * G_PITCH, 8)
                gate_scr[pl.ds(r0, G_HALF), :] = hi | (lo >> 16)
            return carry

        lax.fori_loop(0, tb // 2, pair, 0, unroll=16)

    i0 = e * tiles
    upper = i0 < G_HALF
    r0 = jnp.where(upper, i0, i0 - G_HALF)
    rows = tb // PEER_SPLIT
    for part in range(PEER_SPLIT):
        rs = slice(part * rows, (part + 1) * rows)
        s = _dot(h_ref[rs, :], ut_ref[0])
        words = jnp.concatenate(
            [gate_scr[pl.ds(part * rows * G_PITCH + r0 + il, rows, stride=G_PITCH), :]
             for il in range(tiles)], axis=1)
        bits = jnp.where(upper, words & jnp.uint32(0xFFFF0000), words << 16)
        c = (pltpu.bitcast(bits, F32) * _gelu_tanh(s)).astype(BF16)
        o_ref[rs, :] += _dot(c, v_ref[...])


def _peer_expert(h2, ii, jj, gg, ut, v, tb=PEER_TB, et=PEER_ET):
    T = h2.shape[0]
    ne = PEER_EXPERTS // et
    assert G_HALF % (et // PEER_NKEYS) == 0
    npick = PEER_H * PEER_TOPK
    tspec = pl.BlockSpec((tb, npick), lambda i, e: (i, 0))
    xspec = pl.BlockSpec((tb, D), lambda i, e: (i, 0))
    return pl.pallas_call(
        functools.partial(_peer_expert_kernel, tb=tb, et=et),
        out_shape=jax.ShapeDtypeStruct((T, D), F32),
        grid=(T // tb, ne),
        in_specs=[xspec, tspec, tspec, tspec,
                  pl.BlockSpec((1, D, et), lambda i, e: (e, 0, 0)),
                  pl.BlockSpec((et, D), lambda i, e: (e, 0))],
        out_specs=xspec,
        scratch_shapes=[pltpu.VMEM((tb * G_PITCH, LANES), jnp.uint32)],
        compiler_params=_cparams(("parallel", "arbitrary")),
        name="peer_expert",
    )(h2, ii, jj, gg, ut, v)


def _resid_kernel(*refs, emit_h):
    if emit_h:
        x1_ref, p_ref, gf_ref, g_ref, sc_ref, sh_ref, x_ref, h_ref = refs
    else:
        x1_ref, p_ref, gf_ref, x_ref = refs
    x = x1_ref[...] + gf_ref[0] * p_ref[...]
    x_ref[...] = x
    if emit_h:
        y = x * lax.rsqrt(jnp.mean(x * x, axis=-1, keepdims=True) + EPS) * g_ref[...]
        h_ref[...] = (y * (1.0 + sc_ref[0]) + sh_ref[0]).astype(BF16)


def _resid(x1, peer, mod3, row, next_norm=None, blk0=0, nblk=None):
    nblk = x1.shape[0] // ROW_BLOCK if nblk is None else nblk
    ispec = pl.BlockSpec((ROW_BLOCK, D), lambda i: (blk0 + i, 0))
    ospec = pl.BlockSpec((ROW_BLOCK, D), lambda i: (i, 0))
    mspec = lambda chunk: pl.BlockSpec((1, 1, D), lambda i: (row(blk0 + i), 0, chunk))
    in_specs = [ispec, ispec, mspec(5)]
    args = [x1, peer, mod3]
    out_shape = [jax.ShapeDtypeStruct((nblk * ROW_BLOCK, D), F32)]
    out_specs = [ospec]
    if next_norm is not None:
        g, mod3n = next_norm
        in_specs += [pl.BlockSpec((1, D), lambda i: (0, 0)), mspec(1), mspec(0)]
        args += [g.reshape(1, D), mod3n, mod3n]
        out_shape.append(jax.ShapeDtypeStruct((nblk * ROW_BLOCK, D), BF16))
        out_specs.append(ospec)
    res = pl.pallas_call(
        functools.partial(_resid_kernel, emit_h=next_norm is not None),
        out_shape=out_shape, grid=(nblk,), in_specs=in_specs, out_specs=out_specs,
        compiler_params=_cparams(("parallel",)),
        name="resid",
    )(*args)
    return res if next_norm is not None else (res[0], None)


def _pack_w_in(w):
    o = np.cumsum([0, 256, 256, 512, 512, 32, 384, 128, 64, 512, 512, 512, 512]).tolist()
    z = lambda n: jnp.zeros((D, n), w.dtype)
    w_gla = jnp.concatenate([w[:, o[0]:o[5]], z(GLA_COLS - 1568)], axis=1)
    w_mla = jnp.concatenate([w[:, o[5]:o[8]], z(MLA_COLS - 576)], axis=1)
    w_diff = w[:, o[8]:o[11]]
    w_pool = w[:, o[11]:o[12]]
    return [a.astype(BF16) for a in (w_gla, w_mla, w_diff, w_pool)]


def _pack_w_uq(w):
    w = w.reshape(MLA_Q_RANK, MLA_H, MLA_QK)
    return jnp.concatenate([w[:, :, :MLA_NOPE].reshape(MLA_Q_RANK, -1),
                            w[:, :, MLA_NOPE:].reshape(MLA_Q_RANK, -1)], axis=1).astype(BF16)


def _pack_w_ukv(w):
    w = w.reshape(MLA_KV_RANK, MLA_H, MLA_NOPE + MLA_V)
    return jnp.concatenate([w[:, :, :MLA_NOPE].reshape(MLA_KV_RANK, -1),
                            w[:, :, MLA_NOPE:].reshape(MLA_KV_RANK, -1)], axis=1).astype(BF16)


def kernel(x_prompt, x_sample, state_gla, cache_mla, cache_diff_k, cache_diff_v, c, c_ctx,
           ada_w, ada_b, norm1_g, norm2_g, w_in, w_out, gla_gk_w, gla_gk_b, gla_onorm_g,
           mla_qnorm_g, mla_w_uq, mla_kvnorm_g, mla_w_ukv, mla_q_g, mla_k_g,
           diff_q_g, diff_k_g, diff_lambda, diff_onorm_g, pool_w, pool_scale,
           peer_wq, peer_keys, peer_u, peer_v):
    B, S, _ = x_prompt.shape
    DB, DS, _ = x_sample.shape
    L = ada_w.shape[0]
    t_ctx = B * S
    t_dec = DB * DS
    assert t_ctx % ROW_BLOCK == 0 and DS % ROW_BLOCK == 0 and t_ctx % DS == 0 and DB + 1 <= 16
    row = _mod_row_fn(t_ctx // ROW_BLOCK, DS // ROW_BLOCK)

    cond = jnp.concatenate([c_ctx[None, :], c, jnp.zeros((15 - DB, D), F32)], axis=0)
    mod = _ada_mod(cond, ada_w, ada_b)
    tables = _rope_tables(DS)
    kc_all = cache_diff_k.reshape(DB, L, -1, DIFF_H * LANES)
    vc_all = cache_diff_v.reshape(DB, L, -1, DIFF_H * LANES)

    state_gla_t = jnp.swapaxes(state_gla, -1, -2)
    mods = [mod[l].reshape(16, 1, 6 * D) for l in range(L)]
    T = t_ctx + t_dec
    x, h = _normmod_join(x_prompt.reshape(t_ctx, D), x_sample.reshape(t_dec, D),
                         norm1_g[0], mods[0], row)
    dec_blk0 = t_ctx // DS
    zero_o = lambda: jnp.zeros((T, GROUP_W), BF16)
    st_gla = jnp.zeros((B, L, 2, GLA_H, GLA_DV, GLA_DK), F32)
    st_mla = jnp.zeros((B, L, S, MLA_KV_RANK + MLA_ROPE), F32)
    st_dk = jnp.zeros((B, L, S, DIFF_H * LANES), F32)
    st_dv = jnp.zeros((B, L, S, DIFF_H * LANES), F32)
    for l in range(L):
        mod3 = mods[l]
        w_gla, w_mla, w_diff, w_pool = _pack_w_in(w_in[l])
        p_gla = _matmul(h, w_gla, name="in_gla")
        p_mla = _matmul(h, w_mla, name="in_mla")
        p_diff = _matmul(h, w_diff, name="in_diff")
        p_pool = _matmul(h, w_pool, name="in_pool")

        w_uq = _pack_w_uq(mla_w_uq[l])
        w_ukv = _pack_w_ukv(mla_w_ukv[l])
        gla_w = (gla_gk_w[l], gla_gk_b[l], gla_onorm_g[l])
        mla_w = (mla_qnorm_g[l], w_uq, mla_kvnorm_g[l], w_ukv, mla_q_g[l], mla_k_g[l])
        diff_w = (diff_q_g[l], diff_k_g[l], diff_lambda[l], diff_onorm_g[l])
        pool_wl = pool_w[l].astype(BF16)
        og, st_gla = _gla(p_gla, *gla_w, None, nb=B, S=S, blk0=0, o_prev=zero_o(),
                          st_prev=st_gla, lidx=l, n_layers=L)
        og, _ = _gla(p_gla, *gla_w, state_gla_t, nb=DB, S=DS, blk0=dec_blk0, o_prev=og,
                     st_prev=None, lidx=l, n_layers=L)
        om, st_mla = _mla(p_mla, *mla_w, None, None, nb=B, S=S, blk0=0, lidx=l,
                          o_prev=zero_o(), new_prev=st_mla, n_layers=L)
        om, _ = _mla(p_mla, *mla_w, cache_mla, tables, nb=DB, S=DS, blk0=dec_blk0, lidx=l,
                     o_prev=om, new_prev=None, n_layers=L)
        od, st_dk, st_dv = _diff(p_diff, *diff_w, None, None, None, nb=B, S=S, blk0=0, lidx=l,
                                 o_prev=zero_o(), k_prev=st_dk, v_prev=st_dv, n_layers=L)
        od, _, _ = _diff(p_diff, *diff_w, kc_all, vc_all, tables, nb=DB, S=DS, blk0=dec_blk0, lidx=l,
                         o_prev=od, k_prev=None, v_prev=None, n_layers=L)
        op = _pool(p_pool, pool_wl, pool_scale[l], nb=B, S=S, blk0=0, o_prev=zero_o())
        op = _pool(p_pool, pool_wl, pool_scale[l], nb=DB, S=DS, blk0=dec_blk0, o_prev=op)

        x1, h2 = _outproj(og, om, od, op, w_out[l].astype(BF16), x, mod3, norm2_g[l], row)
        ii, jj, gg = _peer_topk(h2, peer_wq[l].astype(BF16), peer_keys[l].astype(BF16))
        ut = peer_u[l].reshape(PEER_EXPERTS // PEER_ET, PEER_ET, D).transpose(0, 2, 1).astype(BF16)
        peer = _peer_expert(h2, ii, jj, gg, ut, peer_v[l].astype(BF16))
        if l + 1 < L:
            x, h = _resid(x1, peer, mod3, row, (norm1_g[l + 1], mods[l + 1]))

    nc = t_ctx // ROW_BLOCK
    y_prompt, _ = _resid(x1, peer, mod3, row, blk0=0, nblk=nc)
    y_sample, _ = _resid(x1, peer, mod3, row, blk0=nc, nblk=t_dec // ROW_BLOCK)
    return (y_prompt.reshape(B, S, D), y_sample.reshape(DB, DS, D), jnp.swapaxes(st_gla, -1, -2), st_mla,
            st_dk.reshape(B, L, S, DIFF_H, 2 * DIFF_DQK), st_dv.reshape(B, L, S, DIFF_H, DIFF_DV))
```

```python
import functools
import math

import jax
import jax.numpy as jnp
import numpy as np
from jax import lax
from jax.experimental import pallas as pl
from jax.experimental.pallas import tpu as pltpu

F32 = jnp.float32
BF16 = jnp.bfloat16
HI = lax.Precision.HIGHEST

D = 2048
DEPTH = 2
GRID_W = 64
GROUP_W = 512
GLA_H, GLA_DK, GLA_DV, GLA_LR, GLA_NORM, GLA_CHUNK = 4, 64, 128, 16, 16.0, 64
MLA_H, MLA_NOPE, MLA_ROPE, MLA_V, MLA_QK = 4, 128, 64, 128, 192
MLA_Q_RANK, MLA_KV_RANK = 384, 128
DIFF_H, DIFF_DV, DIFF_DQK = 4, 128, 64
POOL_WINDOWS = (2, 4, 8, 16)
POOL_CH = 128
PEER_H, PEER_NKEYS, PEER_TOPK, PEER_DHALF = 8, 128, 16, 128
PEER_EXPERTS = PEER_NKEYS * PEER_NKEYS
ROPE_BASE = 10000.0
EPS = 1e-6

LANES = 128
VMEM_LIMIT = 56 * 1024 * 1024

GLA_COLS = 1664
MLA_COLS = 640
DIFF_COLS = 1536
POOL_COLS = 512

ROW_BLOCK = 256
GLA_NSEQ = 1
PEER_TB = 512
PEER_ET = 1024
PEER_SPLIT = 1
G_HALF = 64
G_PITCH = 72


def _cparams(sem):
    return pltpu.CompilerParams(dimension_semantics=sem, vmem_limit_bytes=VMEM_LIMIT)


def _dot(a, b, precision=None):
    return jnp.dot(a, b, preferred_element_type=F32, precision=precision)


def _dot_nt(a, b, precision=None):
    return lax.dot_general(a, b, (((1,), (1,)), ((), ())), preferred_element_type=F32,
                           precision=precision)


def _dot_tn(a, b, precision=None):
    return lax.dot_general(a, b, (((0,), (0,)), ((), ())), preferred_element_type=F32,
                           precision=precision)


def _silu(x):
    return x / (1.0 + jnp.exp(-x))


def _ada_kernel(c_ref, w_ref, b_ref, o_ref):
    a = _silu(c_ref[...]).astype(BF16)
    o_ref[0] = _dot(a, w_ref[0].astype(BF16)) + b_ref[0]


def _ada_mod(cond16, ada_w, ada_b):
    L, _, N = ada_w.shape
    tn = 1024
    return pl.pallas_call(
        _ada_kernel,
        out_shape=jax.ShapeDtypeStruct((L, 16, N), F32),
        grid=(L, N // tn),
        in_specs=[pl.BlockSpec((16, D), lambda l, j: (0, 0)),
                  pl.BlockSpec((1, D, tn), lambda l, j: (l, 0, j)),
                  pl.BlockSpec((1, 1, tn), lambda l, j: (l, 0, j))],
        out_specs=pl.BlockSpec((1, 16, tn), lambda l, j: (l, 0, j)),
        compiler_params=_cparams(("arbitrary", "arbitrary")),
        name="ada_mod",
    )(cond16, ada_w, ada_b.reshape(L, 1, N))


def _mod_row_fn(n_ctx_blocks, blocks_per_dec):
    def row(i):
        return jnp.where(i < n_ctx_blocks, 0, 1 + (i - n_ctx_blocks) // blocks_per_dec)
    return row


def _mod_spec(row, chunk):
    return pl.BlockSpec((1, 1, D), lambda i: (row(i), 0, chunk))


def _normmod_kernel(xa_ref, xb_ref, g_ref, sc_ref, sh_ref, xo_ref, o_ref, *, na):
    def emit(x_ref):
        x = x_ref[...]
        xo_ref[...] = x
        y = x * lax.rsqrt(jnp.mean(x * x, axis=-1, keepdims=True) + EPS) * g_ref[...]
        o_ref[...] = (y * (1.0 + sc_ref[0]) + sh_ref[0]).astype(BF16)

    @pl.when(pl.program_id(0) < na)
    def _():
        emit(xa_ref)

    @pl.when(pl.program_id(0) >= na)
    def _():
        emit(xb_ref)


def _normmod_join(xa, xb, g, mod3, row):
    na, nb = xa.shape[0] // ROW_BLOCK, xb.shape[0] // ROW_BLOCK
    T = xa.shape[0] + xb.shape[0]
    ospec = pl.BlockSpec((ROW_BLOCK, D), lambda i: (i, 0))
    return pl.pallas_call(
        functools.partial(_normmod_kernel, na=na),
        out_shape=[jax.ShapeDtypeStruct((T, D), F32), jax.ShapeDtypeStruct((T, D), BF16)],
        grid=(na + nb,),
        in_specs=[pl.BlockSpec((ROW_BLOCK, D), lambda i: (jnp.minimum(i, na - 1), 0)),
                  pl.BlockSpec((ROW_BLOCK, D), lambda i: (jnp.maximum(i - na, 0), 0)),
                  pl.BlockSpec((1, D), lambda i: (0, 0)), _mod_spec(row, 1), _mod_spec(row, 0)],
        out_specs=[ospec, ospec],
        compiler_params=_cparams(("parallel",)),
        name="normmod",
    )(xa, xb, g.reshape(1, D), mod3, mod3)


def _mm_kernel(h_ref, w_ref, o_ref):
    o_ref[...] = _dot(h_ref[...], w_ref[...]).astype(o_ref.dtype)


def _matmul(h, w, tm=512, out_dtype=F32, name="matmul"):
    T, K = h.shape
    N = w.shape[1]
    return pl.pallas_call(
        _mm_kernel,
        out_shape=jax.ShapeDtypeStruct((T, N), out_dtype),
        grid=(T // tm,),
        in_specs=[pl.BlockSpec((tm, K), lambda i: (i, 0)),
                  pl.BlockSpec((K, N), lambda i: (0, 0))],
        out_specs=pl.BlockSpec((tm, N), lambda i: (i, 0)),
        compiler_params=_cparams(("parallel",)),
        name=name,
    )(h, w)


def _gla_kernel(*refs, S, has_state, emit_state, unroll):
    if has_state:
        p_ref, gkw_ref, gkb_ref, on_ref, s0_ref = refs[:5]
        rest = refs[5:]
    else:
        p_ref, gkw_ref, gkb_ref, on_ref = refs[:4]
        s0_ref = None
        rest = refs[4:]
    if emit_state:
        o_ref, st_ref, lg_scr, of_scr, st_scr = rest
    else:
        o_ref, lg_scr, of_scr, st_scr = rest
        st_ref = None
    C = GLA_CHUNK
    n = S // C
    H, DK, DV = GLA_H, GLA_DK, GLA_DV
    HK = H * DK
    r_i = lax.broadcasted_iota(jnp.int32, (C, C), 0)
    c_i = lax.broadcasted_iota(jnp.int32, (C, C), 1)
    on_g = on_ref[...]

    for d in (0, 1):
        lr = p_ref[:, 1536 + GLA_LR * d:1536 + GLA_LR * (d + 1)]
        x = _dot(lr, gkw_ref[d], HI) + gkb_ref[d]
        lg_scr[...] = (jnp.minimum(x, 0.0) - jnp.log(1.0 + jnp.exp(-jnp.abs(x)))) * (1.0 / GLA_NORM)
        if d == 0:
            causal = r_i >= c_i
        else:
            causal = r_i <= c_i
        cmat = jnp.where(causal, 1.0, 0.0).astype(BF16)
        for s in range(GLA_NSEQ):
            for h in range(H):
                if has_state:
                    st_scr[s * H + h] = s0_ref[s, 0, d, h]
                else:
                    st_scr[s * H + h] = jnp.zeros((DV, DK), F32)

        def chunk(ci, carry, d=d, cmat=cmat, causal=causal):
            c = ci if d == 0 else n - 1 - ci
            for s in range(GLA_NSEQ):
                r0 = pl.multiple_of(s * S + c * C, C)
                gc = lg_scr[pl.ds(r0, C), :]
                g1 = gc.astype(BF16)
                rem = gc - g1.astype(F32)
                g2 = rem.astype(BF16)
                g3 = (rem - g2.astype(F32)).astype(BF16)
                cs = _dot(cmat, jnp.concatenate([g1, g2, g3], axis=1))
                cum = cs[:, :HK] + cs[:, HK:2 * HK] + cs[:, 2 * HK:]
                q = p_ref[pl.ds(r0, C), 0:256] * (DK ** -0.5)
                k = p_ref[pl.ds(r0, C), 256:512]
                qe = (q * jnp.exp(cum)).astype(BF16)
                ke = (k * jnp.exp(-cum)).astype(BF16)
                tot = jnp.sum(gc, axis=0, keepdims=True)
                kd = (k * jnp.exp(tot - cum)).astype(BF16)
                dec = jnp.exp(tot)
                outs = []
                for h in range(H):
                    sl = slice(h * DK, (h + 1) * DK)
                    v_h = p_ref[pl.ds(r0, C), 512 + h * DV:512 + (h + 1) * DV].astype(BF16)
                    st = st_scr[s * H + h]
                    att = _dot_nt(qe[:, sl], ke[:, sl])
                    att = jnp.where(causal, att, 0.0).astype(BF16)
                    o_h = _dot_nt(qe[:, sl], st.astype(BF16)) + _dot(att, v_h)
                    st_scr[s * H + h] = dec[:, sl] * st + _dot_tn(v_h, kd[:, sl])
                    outs.append(o_h)
                o_c = jnp.concatenate(outs, axis=1)
                if d == 0:
                    of_scr[pl.ds(r0, C), :] = o_c
                else:
                    o_c = o_c + of_scr[pl.ds(r0, C), :]
                    gate = p_ref[pl.ds(r0, C), 1024:1536]
                    res = []
                    for h in range(H):
                        oh = o_c[:, h * DV:(h + 1) * DV]
                        oh = oh * lax.rsqrt(jnp.mean(oh * oh, axis=-1, keepdims=True) + EPS) * on_g
                        res.append(oh * _silu(gate[:, h * DV:(h + 1) * DV]))
                    o_ref[pl.ds(r0, C), :] = jnp.concatenate(res, axis=1).astype(o_ref.dtype)
            return carry

        lax.fori_loop(0, n, chunk, 0, unroll=unroll)
        if emit_state:
            for s in range(GLA_NSEQ):
                for h in range(H):
                    st_ref[s, 0, d, h] = st_scr[s * H + h]


def _carried_call(kern, *, grid, in_specs, args, outs, scratch_shapes, name):
    n_in = len(args)
    carried = [(k, o[2]) for k, o in enumerate(outs) if o[2] is not None]
    n_carried = len(carried)

    def body(*refs):
        kern(*refs[:n_in], *refs[n_in + n_carried:])

    return pl.pallas_call(
        body,
        out_shape=[o[0] for o in outs], grid=grid,
        in_specs=list(in_specs) + [pl.BlockSpec(memory_space=pl.ANY)] * n_carried,
        out_specs=[o[1] for o in outs],
        scratch_shapes=scratch_shapes,
        input_output_aliases={n_in + j: k for j, (k, _) in enumerate(carried)},
        compiler_params=_cparams(("parallel",)),
        name=name,
    )(*args, *[prev for _, prev in carried])


def _gla(proj, gk_w, gk_b, onorm_g, s0, *, nb, S, blk0, st_prev, lidx, n_layers):
    has_state = s0 is not None
    emit_state = not has_state
    ns = GLA_NSEQ
    assert nb % ns == 0 and blk0 % ns == 0
    gb0 = blk0 // ns
    rows = ns * S
    in_specs = [pl.BlockSpec((rows, GLA_COLS), lambda b: (gb0 + b, 0)),
                pl.BlockSpec((2, GLA_LR, GLA_H * GLA_DK), lambda b: (0, 0, 0)),
                pl.BlockSpec((2, 1, GLA_H * GLA_DK), lambda b: (0, 0, 0)),
                pl.BlockSpec((1, GLA_DV), lambda b: (0, 0))]
    args = [proj, gk_w, gk_b.reshape(2, 1, -1), onorm_g.reshape(1, -1)]
    st_dims = (2, GLA_H, GLA_DV, GLA_DK)
    if has_state:
        in_specs.append(pl.BlockSpec((ns, 1) + st_dims, lambda b: (b, lidx, 0, 0, 0, 0)))
        args.append(s0)
    outs = [(jax.ShapeDtypeStruct((nb * S, GLA_H * GLA_DV), BF16),
             pl.BlockSpec((rows, GLA_H * GLA_DV), lambda b: (b, 0)), None)]
    if emit_state:
        outs.append((jax.ShapeDtypeStruct((nb, n_layers) + st_dims, F32),
                     pl.BlockSpec((ns, 1) + st_dims, lambda b: (b, lidx, 0, 0, 0, 0)), st_prev))
    res = _carried_call(
        functools.partial(_gla_kernel, S=S, has_state=has_state, emit_state=emit_state,
                          unroll=4 // ns),
        grid=(nb // ns,), in_specs=in_specs, args=args, outs=outs,
        scratch_shapes=[pltpu.VMEM((rows, GLA_H * GLA_DK), F32),
                        pltpu.VMEM((rows, GLA_H * GLA_DV), F32),
                        pltpu.VMEM((ns * GLA_H, GLA_DV, GLA_DK), F32)],
        name="gla_dec" if has_state else "gla_ctx")
    return res if emit_state else (res[0], None)


def _rope_tables(n_tokens):
    t = jnp.arange(n_tokens)
    row = (t // GRID_W).astype(F32)
    col = (t % GRID_W).astype(F32)
    half = 16
    freq = ROPE_BASE ** (-jnp.arange(half, dtype=F32) / half)
    ar = row[:, None] * freq[None, :]
    ac = col[:, None] * freq[None, :]
    cos = jnp.concatenate([jnp.cos(ar), jnp.cos(ar), jnp.cos(ac), jnp.cos(ac)], axis=1)
    sin = jnp.concatenate([-jnp.sin(ar), jnp.sin(ar), -jnp.sin(ac), jnp.sin(ac)], axis=1)
    return jnp.tile(cos, (1, 2)), jnp.tile(sin, (1, 2))


def _rope128(x, cos, sin):
    lane = lax.broadcasted_iota(jnp.int32, x.shape, 1)
    first = (lane % 32) < 16
    partner = jnp.where(first, pltpu.roll(x, LANES - 16, 1), pltpu.roll(x, 16, 1))
    return x * cos + partner * sin


def _mla_kernel(*refs, S, P, rope, qb):
    if P:
        (p_ref, qn_ref, wuq_ref, kvn_ref, wukv_ref, qg_ref, kg_ref, cache_ref, cos_ref, sin_ref,
         o_ref, k_scr, v_scr) = refs
        new_ref = None
    else:
        (p_ref, qn_ref, wuq_ref, kvn_ref, wukv_ref, qg_ref, kg_ref,
         o_ref, new_ref, k_scr, v_scr) = refs
    H = MLA_H
    NK = P + S
    qg = qg_ref[...]
    kg = kg_ref[...]
    qg_n, qg_r = qg[:, :MLA_NOPE], qg[:, MLA_NOPE:]
    kg_n, kg_r = kg[:, :MLA_NOPE], kg[:, MLA_NOPE:]
    qg_r2 = jnp.concatenate([qg_r, qg_r], axis=1)
    kg_r2 = jnp.concatenate([kg_r, kg_r], axis=1)

    ckv = p_ref[:, 384:512]
    ckv = ckv * lax.rsqrt(jnp.mean(ckv * ckv, axis=-1, keepdims=True) + EPS) * kvn_ref[...]
    kpe = p_ref[:, 512:576]
    if new_ref is not None:
        new_ref[0, 0, :, 0:MLA_KV_RANK] = ckv
        new_ref[0, 0, :, MLA_KV_RANK:] = kpe

    def expand(c, pe, do_rope, r0, nrows):
        kv = _dot(c.astype(BF16), wukv_ref[...])
        pe2 = jnp.concatenate([pe, pe], axis=1)
        pe_ss = jnp.sum(pe * pe, axis=-1, keepdims=True)
        pe_g = pe2 * kg_r2
        if do_rope:
            pe_g = _rope128(pe_g, cos_ref[...], sin_ref[...])
        for h in range(H):
            kn = kv[:, h * MLA_NOPE:(h + 1) * MLA_NOPE]
            r = lax.rsqrt((jnp.sum(kn * kn, axis=-1, keepdims=True) + pe_ss) / MLA_QK + EPS)
            kfull = jnp.concatenate([kn * kg_n * r, pe_g[:, :MLA_ROPE] * r], axis=1)
            k_scr[h, r0:r0 + nrows, :] = kfull.astype(BF16)
            v_scr[h, r0:r0 + nrows, :] = jnp.concatenate(
                [kv[:, 512 + h * MLA_V:512 + (h + 1) * MLA_V].astype(BF16),
                 jnp.ones((nrows, MLA_V), BF16)], axis=1)

    if P:
        cc = cache_ref[0, 0]
        expand(cc[:, :MLA_KV_RANK], cc[:, MLA_KV_RANK:], False, 0, P)
    expand(ckv, kpe, rope, P, S)

    cq = p_ref[:, 0:384]
    cq = cq * lax.rsqrt(jnp.mean(cq * cq, axis=-1, keepdims=True) + EPS) * qn_ref[...]
    qall = _dot(cq.astype(BF16), wuq_ref[...])
    scale = MLA_QK ** -0.5
    lane = lax.broadcasted_iota(jnp.int32, (S, LANES), 1)
    for hp in range(H // 2):
        rs = []
        qns = []
        for j in range(2):
            h = 2 * hp + j
            qn = qall[:, h * MLA_NOPE:(h + 1) * MLA_NOPE]
            qr = qall[:, 512 + h * MLA_ROPE:512 + (h + 1) * MLA_ROPE]
            ss = jnp.sum(qn * qn, axis=-1, keepdims=True) + jnp.sum(qr * qr, axis=-1, keepdims=True)
            r = lax.rsqrt(ss / MLA_QK + EPS)
            rs.append(r)
            qns.append(qn * qg_n * r)
        qr2 = qall[:, 512 + hp * LANES:512 + (hp + 1) * LANES] * qg_r2
        qr2 = qr2 * jnp.where(lane < MLA_ROPE, rs[0], rs[1])
        if rope:
            qr2 = _rope128(qr2, cos_ref[...], sin_ref[...])
        for j in range(2):
            h = 2 * hp + j
            qfull = jnp.concatenate([qns[j], qr2[:, j * MLA_ROPE:(j + 1) * MLA_ROPE]], axis=1)
            qfull = (qfull * scale).astype(BF16)
            kh = k_scr[h]
            vh = v_scr[h]
            for i in range(S // qb):
                sc = _dot_nt(qfull[i * qb:(i + 1) * qb], kh)
                m = jnp.max(sc, axis=-1, keepdims=True)
                e = jnp.exp(sc - m)
                ov = _dot(e.astype(BF16), vh)
                o = ov[:, :MLA_V] / ov[:, MLA_V:]
                o_ref[i * qb:(i + 1) * qb, h * MLA_V:(h + 1) * MLA_V] = o.astype(o_ref.dtype)


def _mla(proj, qnorm_g, w_uq, kvnorm_g, w_ukv, q_g, k_g, cache, tables, *, nb, S, blk0, lidx,
         new_prev, n_layers):
    P = 0 if cache is None else cache.shape[2]
    const2 = lambda b: (0, 0)
    in_specs = [pl.BlockSpec((S, MLA_COLS), lambda b: (blk0 + b, 0)),
                pl.BlockSpec((1, MLA_Q_RANK), const2),
                pl.BlockSpec((MLA_Q_RANK, MLA_H * MLA_QK), const2),
                pl.BlockSpec((1, MLA_KV_RANK), const2),
                pl.BlockSpec((MLA_KV_RANK, MLA_H * (MLA_NOPE + MLA_V)), const2),
                pl.BlockSpec((1, MLA_QK), const2),
                pl.BlockSpec((1, MLA_QK), const2)]
    args = [proj, qnorm_g.reshape(1, -1), w_uq, kvnorm_g.reshape(1, -1), w_ukv,
            q_g.reshape(1, -1), k_g.reshape(1, -1)]
    if P:
        in_specs += [pl.BlockSpec((1, 1, P, MLA_KV_RANK + MLA_ROPE), lambda b: (b, lidx, 0, 0)),
                     pl.BlockSpec((S, LANES), const2), pl.BlockSpec((S, LANES), const2)]
        args += [cache, tables[0], tables[1]]
    outs = [(jax.ShapeDtypeStruct((nb * S, MLA_H * MLA_V), BF16),
             pl.BlockSpec((S, MLA_H * MLA_V), lambda b: (b, 0)), None)]
    if not P:
        cw = MLA_KV_RANK + MLA_ROPE
        outs.append((jax.ShapeDtypeStruct((nb, n_layers, S, cw), F32),
                     pl.BlockSpec((1, 1, S, cw), lambda b: (b, lidx, 0, 0)), new_prev))
    res = _carried_call(
        functools.partial(_mla_kernel, S=S, P=P, rope=bool(P), qb=min(S, 256)),
        grid=(nb,), in_specs=in_specs, args=args, outs=outs,
        scratch_shapes=[pltpu.VMEM((MLA_H, P + S, MLA_QK), BF16),
                        pltpu.VMEM((MLA_H, P + S, 2 * MLA_V), BF16)],
        name="mla_dec" if P else "mla_ctx")
    return (res[0], None) if P else res


def _group_rms64(x, g2):
    lane = lax.broadcasted_iota(jnp.int32, x.shape, 1)
    lo = lane < DIFF_DQK
    xx = x * x
    s_all = jnp.sum(xx, axis=-1, keepdims=True)
    s_lo = jnp.sum(jnp.where(lo, xx, 0.0), axis=-1, keepdims=True)
    ms = jnp.where(lo, s_lo, s_all - s_lo) * (1.0 / DIFF_DQK)
    return x * lax.rsqrt(ms + EPS) * g2


def _diff_kernel(*refs, S, P, rope, qb, lam_init):
    if P:
        (p_ref, qg_ref, kg_ref, lam_ref, on_ref, kc_ref, vc_ref, cos_ref, sin_ref,
         o_ref, k_scr, v_scr) = refs
        ko_ref = vo_ref = None
    else:
        (p_ref, qg_ref, kg_ref, lam_ref, on_ref, o_ref, ko_ref, vo_ref, k_scr, v_scr) = refs
    H = DIFF_H
    qg2 = jnp.concatenate([qg_ref[...], qg_ref[...]], axis=1)
    kg2 = jnp.concatenate([kg_ref[...], kg_ref[...]], axis=1)
    lp = lam_ref[...]
    l1 = jnp.sum(jnp.sum(lp[0:1] * lp[1:2], axis=-1, keepdims=True), axis=0, keepdims=True)
    l2 = jnp.sum(jnp.sum(lp[2:3] * lp[3:4], axis=-1, keepdims=True), axis=0, keepdims=True)
    lam = jnp.exp(l1) - jnp.exp(l2) + lam_init
    scale = DIFF_DQK ** -0.5
    lane = lax.broadcasted_iota(jnp.int32, (S, LANES), 1)
    lo = lane < DIFF_DQK
    on_g = on_ref[...]

    if vo_ref is not None:
        vo_ref[0, 0] = p_ref[:, 1024:1536]
    for h in range(H):
        sl = slice(h * LANES, (h + 1) * LANES)
        kn = _group_rms64(p_ref[:, 512 + h * LANES:512 + (h + 1) * LANES], kg2)
        if ko_ref is not None:
            ko_ref[0, 0, :, sl] = kn
        if rope:
            kn = _rope128(kn, cos_ref[...], sin_ref[...])
        if P:
            k_scr[0:P, :] = kc_ref[0, 0, :, sl].astype(BF16)
            v_scr[0:P, :] = vc_ref[0, 0, :, sl].astype(BF16)
        k_scr[P:P + S, :] = kn.astype(BF16)
        v_scr[P:P + S, :] = p_ref[:, 1024 + h * LANES:1024 + (h + 1) * LANES].astype(BF16)
        qn = _group_rms64(p_ref[:, sl], qg2)
        if rope:
            qn = _rope128(qn, cos_ref[...], sin_ref[...])
        qn = qn * scale
        q0 = jnp.where(lo, qn, 0.0).astype(BF16)
        q1 = jnp.where(lo, 0.0, qn).astype(BF16)
        kh = k_scr[...]
        vh = v_scr[...]
        for i in range(S // qb):
            rows = slice(i * qb, (i + 1) * qb)
            ps = []
            for qq in (q0, q1):
                sc = _dot_nt(qq[rows], kh)
                m = jnp.max(sc, axis=-1, keepdims=True)
                e = jnp.exp(sc - m)
                ps.append(e / jnp.sum(e, axis=-1, keepdims=True))
            a = (ps[0] - lam * ps[1]).astype(BF16)
            o = _dot(a, vh)
            o = o * lax.rsqrt(jnp.mean(o * o, axis=-1, keepdims=True) + EPS) * on_g
            o_ref[rows, sl] = (o * (1.0 - lam_init)).astype(o_ref.dtype)


def _diff(proj, q_g, k_g, lam_p, onorm_g, kc, vc, tables, *, nb, S, blk0, lidx,
          k_prev, v_prev, n_layers):
    P = 0 if kc is None else kc.shape[2]
    const2 = lambda b: (0, 0)
    in_specs = [pl.BlockSpec((S, DIFF_COLS), lambda b: (blk0 + b, 0)),
                pl.BlockSpec((1, DIFF_DQK), const2), pl.BlockSpec((1, DIFF_DQK), const2),
                pl.BlockSpec((4, DIFF_DQK), const2), pl.BlockSpec((1, DIFF_DV), const2)]
    args = [proj, q_g.reshape(1, -1), k_g.reshape(1, -1), lam_p, onorm_g.reshape(1, -1)]
    if P:
        cspec = pl.BlockSpec((1, 1, P, DIFF_H * LANES), lambda b: (b, lidx, 0, 0))
        in_specs += [cspec, cspec, pl.BlockSpec((S, LANES), const2), pl.BlockSpec((S, LANES), const2)]
        args += [kc, vc, tables[0], tables[1]]
    lam_init = 0.8 - 0.6 * math.exp(-0.3 * lidx)
    outs = [(jax.ShapeDtypeStruct((nb * S, DIFF_H * DIFF_DV), BF16),
             pl.BlockSpec((S, DIFF_H * DIFF_DV), lambda b: (b, 0)), None)]
    if not P:
        cshape = jax.ShapeDtypeStruct((nb, n_layers, S, DIFF_H * LANES), F32)
        cspec = pl.BlockSpec((1, 1, S, DIFF_H * LANES), lambda b: (b, lidx, 0, 0))
        outs += [(cshape, cspec, k_prev), (cshape, cspec, v_prev)]
    res = _carried_call(
        functools.partial(_diff_kernel, S=S, P=P, rope=bool(P), qb=min(S, 256), lam_init=lam_init),
        grid=(nb,), in_specs=in_specs, args=args, outs=outs,
        scratch_shapes=[pltpu.VMEM((P + S, LANES), BF16), pltpu.VMEM((P + S, LANES), BF16)],
        name="diff_dec" if P else "diff_ctx")
    return (res[0], None, None) if P else res


POOL_PAD = 8


def _pool_kernel(p_ref, w_ref, sc_ref, o_ref, pad_scr, *, S):
    zeros = jnp.zeros((POOL_PAD, GROUP_W), F32)
    pad_scr[0:POOL_PAD, :] = zeros
    pad_scr[POOL_PAD + S:POOL_PAD + S + POOL_PAD, :] = zeros
    pad_scr[POOL_PAD:POOL_PAD + S, :] = p_ref[...]
    t = lax.broadcasted_iota(jnp.int32, (S, POOL_CH), 0)
    for gi, win in enumerate(POOL_WINDOWS):
        cs = slice(gi * POOL_CH, (gi + 1) * POOL_CH)
        acc = None
        for dlt in range(-(win // 2), win // 2):
            piece = pad_scr[POOL_PAD + dlt:POOL_PAD + dlt + S, cs]
            acc = piece if acc is None else acc + piece
        cnt = (jnp.minimum(t + win // 2, S) - jnp.maximum(t - win // 2, 0)).astype(F32)
        y = (acc / cnt - p_ref[:, cs]).astype(BF16)
        o_ref[:, cs] = (_dot(y, w_ref[gi]) * sc_ref[:, cs]).astype(o_ref.dtype)


def _pool(proj, w, scale, *, nb, S, blk0):
    return _carried_call(
        functools.partial(_pool_kernel, S=S),
        grid=(nb,),
        in_specs=[pl.BlockSpec((S, POOL_COLS), lambda b: (blk0 + b, 0)),
                  pl.BlockSpec((len(POOL_WINDOWS), POOL_CH, POOL_CH), lambda b: (0, 0, 0)),
                  pl.BlockSpec((1, GROUP_W), lambda b: (0, 0))],
        args=[proj, w, scale.reshape(1, -1)],
        outs=[(jax.ShapeDtypeStruct((nb * S, GROUP_W), BF16),
               pl.BlockSpec((S, GROUP_W), lambda b: (b, 0)), None)],
        scratch_shapes=[pltpu.VMEM((S + 2 * POOL_PAD, GROUP_W), F32)],
        name="pool")[0]


def _outproj_kernel(*refs, na):
    ctx_refs, dec_refs = refs[0:4], refs[4:8]
    w_ref, x_ref, ga_ref, g2_ref, sc_ref, sh_ref, x1_ref, h2_ref = refs[8:]
    is_ctx = pl.program_id(0) < na
    mix = None
    for j in range(4):
        o_j = jnp.where(is_ctx, ctx_refs[j][...], dec_refs[j][...])
        part = _dot(o_j, w_ref[j * GROUP_W:(j + 1) * GROUP_W, :])
        mix = part if mix is None else mix + part
    x1 = x_ref[...] + ga_ref[0] * mix
    x1_ref[...] = x1
    y = x1 * lax.rsqrt(jnp.mean(x1 * x1, axis=-1, keepdims=True) + EPS) * g2_ref[...]
    h2_ref[...] = (y * (1.0 + sc_ref[0]) + sh_ref[0]).astype(BF16)


def _outproj(o_ctx, o_dec, w_out, x, mod3, norm2_g, row):
    T = x.shape[0]
    na = o_ctx[0].shape[0] // ROW_BLOCK
    cspec = pl.BlockSpec((ROW_BLOCK, GROUP_W), lambda i: (jnp.minimum(i, na - 1), 0))
    dspec = pl.BlockSpec((ROW_BLOCK, GROUP_W), lambda i: (jnp.maximum(i - na, 0), 0))
    xspec = pl.BlockSpec((ROW_BLOCK, D), lambda i: (i, 0))
    return pl.pallas_call(
        functools.partial(_outproj_kernel, na=na),
        out_shape=[jax.ShapeDtypeStruct((T, D), F32), jax.ShapeDtypeStruct((T, D), BF16)],
        grid=(T // ROW_BLOCK,),
        in_specs=[cspec] * 4 + [dspec] * 4 + [
                  pl.BlockSpec((D, D), lambda i: (0, 0)),
                  xspec, _mod_spec(row, 2),
                  pl.BlockSpec((1, D), lambda i: (0, 0)),
                  _mod_spec(row, 4), _mod_spec(row, 3)],
        out_specs=[xspec, xspec],
        compiler_params=_cparams(("parallel",)),
        name="outproj",
    )(*o_ctx, *o_dec, w_out, x, mod3, norm2_g.reshape(1, D), mod3, mod3)


def _topk_chains(x_refs, nrows, tt):
    rows = lax.broadcasted_iota(jnp.int32, (nrows, tt), 0)
    krow = lax.broadcasted_iota(jnp.int32, (PEER_TOPK, tt), 0)
    ng = nrows // 8
    rows_f = rows.astype(F32)
    sub = lax.broadcasted_iota(jnp.int32, (8, tt), 0).astype(F32)
    row_groups = [sub + float(8 * g) for g in range(ng)]

    def argmax_rows(x):
        items = [(x[8 * g:8 * (g + 1)], row_groups[g]) for g in range(ng)]
        while len(items) > 1:
            nxt = []
            for j in range(0, len(items) - 1, 2):
                (a, ra), (b, rb) = items[j], items[j + 1]
                gt = b > a
                nxt.append((jnp.where(gt, b, a), jnp.where(gt, rb, ra)))
            if len(items) % 2:
                nxt.append(items[-1])
            items = nxt
        v, r = items[0]
        m = jnp.max(v, axis=0, keepdims=True)
        pos = jnp.min(jnp.where(v == m, r, float(nrows)), axis=0, keepdims=True)
        return m, pos

    def body(k, carry):
        out = []
        for x_ref, (vals, sel) in zip(x_refs, carry):
            x = x_ref[...]
            m, pos = argmax_rows(x)
            x_ref[...] = jnp.where(rows_f == pos, -jnp.inf, x)
            out.append((jnp.where(krow == k, m, vals), jnp.where(krow == k, pos, sel)))
        return tuple(out)

    init = (jnp.zeros((PEER_TOPK, tt), F32), jnp.zeros((PEER_TOPK, tt), F32))
    res = lax.fori_loop(0, PEER_TOPK, body, tuple(init for _ in x_refs))
    return [(vals, sel.astype(jnp.int32)) for vals, sel in res]


CAND_ROWS = 80


def _cand_ab(r):
    a = jnp.where(r < 16, 0, jnp.where(r < 72, 1 + ((r - 16) >> 3), r - 64))
    b = jnp.where(r < 16, r, jnp.where(r < 72, (r - 16) & 7, 0))
    return a, b


def _peer_topk_kernel(h_ref, wq_ref, keys_ref, i_ref, j_ref, g_ref,
                      q_scr, sc_scr, cand_scr, i_scr, j_scr, g_scr, *, tt):
    q = _dot(h_ref[...], wq_ref[...]).astype(BF16)
    for hp in range(2 * PEER_H):
        q_scr[hp] = q[:, hp * PEER_DHALF:(hp + 1) * PEER_DHALF]
    K = PEER_TOPK

    def head_pair(hh, carry):
        for u in range(2):
            for p in range(2):
                h = 2 * hh + u
                sc_scr[2 * u + p] = _dot_nt(keys_ref[h, p], q_scr[2 * h + p])
        tops = _topk_chains([sc_scr.at[c] for c in range(4)], PEER_NKEYS, tt)
        for u in range(2):
            (s0, _), (s1, _) = tops[2 * u], tops[2 * u + 1]
            cand_scr[u, 0:K, :] = s0[0:1, :] + s1
            for a in range(1, 8):
                cand_scr[u, 8 + 8 * a:16 + 8 * a, :] = s0[a:a + 1, :] + s1[0:8, :]
            cand_scr[u, 72:80, :] = s0[8:16, :] + s1[0:1, :]
        picks = _topk_chains([cand_scr.at[u] for u in range(2)], CAND_ROWS, tt)
        for u in range(2):
            (_, i0), (_, i1) = tops[2 * u], tops[2 * u + 1]
            best, pos = picks[u]
            a_sel, b_sel = _cand_ab(pos)
            i_sel = jnp.zeros((K, tt), jnp.int32)
            j_sel = jnp.zeros((K, tt), jnp.int32)
            for a in range(K):
                i_sel = jnp.where(a_sel == a, i0[a:a + 1, :], i_sel)
                j_sel = jnp.where(b_sel == a, i1[a:a + 1, :], j_sel)
            e = jnp.exp(best - best[0:1, :])
            i_scr[2 * hh + u] = i_sel.astype(F32)
            j_scr[2 * hh + u] = j_sel.astype(F32)
            g_scr[2 * hh + u] = e / jnp.sum(e, axis=0, keepdims=True)
        return carry

    lax.fori_loop(0, PEER_H // 2, head_pair, 0)
    nrow = PEER_H * K
    i_ref[...] = i_scr[...].reshape(nrow, tt).T.astype(jnp.int32)
    j_ref[...] = j_scr[...].reshape(nrow, tt).T.astype(jnp.int32)
    g_ref[...] = g_scr[...].reshape(nrow, tt).T


def _peer_topk(h2, wq, keys, tt=ROW_BLOCK):
    T = h2.shape[0]
    npick = PEER_H * PEER_TOPK
    ospec = pl.BlockSpec((tt, npick), lambda i: (i, 0))
    return pl.pallas_call(
        functools.partial(_peer_topk_kernel, tt=tt),
        out_shape=[jax.ShapeDtypeStruct((T, npick), jnp.int32),
                   jax.ShapeDtypeStruct((T, npick), jnp.int32),
                   jax.ShapeDtypeStruct((T, npick), F32)],
        grid=(T // tt,),
        in_specs=[pl.BlockSpec((tt, D), lambda i: (i, 0)),
                  pl.BlockSpec((D, D), lambda i: (0, 0)),
                  pl.BlockSpec((PEER_H, 2, PEER_NKEYS, PEER_DHALF), lambda i: (0, 0, 0, 0))],
        out_specs=[ospec, ospec, ospec],
        scratch_shapes=[pltpu.VMEM((2 * PEER_H, tt, PEER_DHALF), BF16),
                        pltpu.VMEM((4, PEER_NKEYS, tt), F32),
                        pltpu.VMEM((2, CAND_ROWS, tt), F32),
                        pltpu.VMEM((PEER_H, PEER_TOPK, tt), F32),
                        pltpu.VMEM((PEER_H, PEER_TOPK, tt), F32),
                        pltpu.VMEM((PEER_H, PEER_TOPK, tt), F32)],
        compiler_params=_cparams(("parallel",)),
        name="peer_topk",
    )(h2, wq, keys)


def _gelu_tanh(x):
    return 0.5 * x * (1.0 + jnp.tanh(0.7978845608028654 * (x + 0.044715 * x * x * x)))


def _peer_expert_kernel(h_ref, i_ref, j_ref, g_ref, ut_ref, v_ref, o_ref, gate_scr, *, tb, et):
    e = pl.program_id(1)
    nk = PEER_NKEYS
    npick = PEER_H * PEER_TOPK
    tiles = et // nk

    @pl.when(e == 0)
    def _():
        o_ref[...] = jnp.zeros_like(o_ref)
        sub = lax.broadcasted_iota(jnp.int32, (nk, npick), 0)
        zero = jnp.zeros((nk, npick), BF16)

        def pair(tp, carry):
            lhs, rhs = [], []
            for u in range(2):
                t = 2 * tp + u
                irow = i_ref[pl.ds(t, 1), :]
                jrow = j_ref[pl.ds(t, 1), :]
                grow = g_ref[pl.ds(t, 1), :]
                lhs.append(jnp.where(sub == irow, 1.0, 0.0).astype(BF16))
                xt = jnp.where(sub == jrow, grow, 0.0).astype(BF16)
                rhs.append(jnp.concatenate([xt, zero] if u == 0 else [zero, xt], axis=1))
            g2 = _dot_nt(jnp.concatenate(lhs, axis=1), jnp.concatenate(rhs, axis=0))
            for u in range(2):
                g = g2[:, u * nk:(u + 1) * nk].astype(BF16).astype(F32)
                hi = pltpu.bitcast(g[:G_HALF], jnp.uint32)
                lo = pltpu.bitcast(g[G_HALF:], jnp.uint32)
                r0 = pl.multiple_of((2 * tp + u) * G_PITCH, 8)
                gate_scr[pl.ds(r0, G_HALF), :] = hi | (lo >> 16)
            return carry

        lax.fori_loop(0, tb // 2, pair, 0, unroll=16)

    i0 = e * tiles
    upper = i0 < G_HALF
    r0 = jnp.where(upper, i0, i0 - G_HALF)
    rows = tb // PEER_SPLIT
    for part in range(PEER_SPLIT):
        rs = slice(part * rows, (part + 1) * rows)
        s = _dot(h_ref[rs, :], ut_ref[0])
        words = jnp.concatenate(
            [gate_scr[pl.ds(part * rows * G_PITCH + r0 + il, rows, stride=G_PITCH), :]
             for il in range(tiles)], axis=1)
        bits = jnp.where(upper, words & jnp.uint32(0xFFFF0000), words << 16)
        c = (pltpu.bitcast(bits, F32) * _gelu_tanh(s)).astype(BF16)
        o_ref[rs, :] += _dot(c, v_ref[...])


def _peer_expert(h2, ii, jj, gg, ut, v, tb=PEER_TB, et=PEER_ET):
    T = h2.shape[0]
    ne = PEER_EXPERTS // et
    assert G_HALF % (et // PEER_NKEYS) == 0
    npick = PEER_H * PEER_TOPK
    tspec = pl.BlockSpec((tb, npick), lambda i, e: (i, 0))
    xspec = pl.BlockSpec((tb, D), lambda i, e: (i, 0))
    return pl.pallas_call(
        functools.partial(_peer_expert_kernel, tb=tb, et=et),
        out_shape=jax.ShapeDtypeStruct((T, D), F32),
        grid=(T // tb, ne),
        in_specs=[xspec, tspec, tspec, tspec,
                  pl.BlockSpec((1, D, et), lambda i, e: (e, 0, 0)),
                  pl.BlockSpec((et, D), lambda i, e: (e, 0))],
        out_specs=xspec,
        scratch_shapes=[pltpu.VMEM((tb * G_PITCH, LANES), jnp.uint32)],
        compiler_params=_cparams(("parallel", "arbitrary")),
        name="peer_expert",
    )(h2, ii, jj, gg, ut, v)


def _resid_kernel(*refs, emit_h):
    if emit_h:
        x1_ref, p_ref, gf_ref, g_ref, sc_ref, sh_ref, x_ref, h_ref = refs
    else:
        x1_ref, p_ref, gf_ref, x_ref = refs
    x = x1_ref[...] + gf_ref[0] * p_ref[...]
    x_ref[...] = x
    if emit_h:
        y = x * lax.rsqrt(jnp.mean(x * x, axis=-1, keepdims=True) + EPS) * g_ref[...]
        h_ref[...] = (y * (1.0 + sc_ref[0]) + sh_ref[0]).astype(BF16)


def _resid(x1, peer, mod3, row, next_norm=None, blk0=0, nblk=None):
    nblk = x1.shape[0] // ROW_BLOCK if nblk is None else nblk
    ispec = pl.BlockSpec((ROW_BLOCK, D), lambda i: (blk0 + i, 0))
    ospec = pl.BlockSpec((ROW_BLOCK, D), lambda i: (i, 0))
    mspec = lambda chunk: pl.BlockSpec((1, 1, D), lambda i: (row(blk0 + i), 0, chunk))
    in_specs = [ispec, ispec, mspec(5)]
    args = [x1, peer, mod3]
    out_shape = [jax.ShapeDtypeStruct((nblk * ROW_BLOCK, D), F32)]
    out_specs = [ospec]
    if next_norm is not None:
        g, mod3n = next_norm
        in_specs += [pl.BlockSpec((1, D), lambda i: (0, 0)), mspec(1), mspec(0)]
        args += [g.reshape(1, D), mod3n, mod3n]
        out_shape.append(jax.ShapeDtypeStruct((nblk * ROW_BLOCK, D), BF16))
        out_specs.append(ospec)
    res = pl.pallas_call(
        functools.partial(_resid_kernel, emit_h=next_norm is not None),
        out_shape=out_shape, grid=(nblk,), in_specs=in_specs, out_specs=out_specs,
        compiler_params=_cparams(("parallel",)),
        name="resid",
    )(*args)
    return res if next_norm is not None else (res[0], None)


def _pack_w_in(w):
    o = np.cumsum([0, 256, 256, 512, 512, 32, 384, 128, 64, 512, 512, 512, 512]).tolist()
    z = lambda n: jnp.zeros((D, n), w.dtype)
    w_gla = jnp.concatenate([w[:, o[0]:o[5]], z(GLA_COLS - 1568)], axis=1)
    w_mla = jnp.concatenate([w[:, o[5]:o[8]], z(MLA_COLS - 576)], axis=1)
    w_diff = w[:, o[8]:o[11]]
    w_pool = w[:, o[11]:o[12]]
    return [a.astype(BF16) for a in (w_gla, w_mla, w_diff, w_pool)]


def _pack_w_uq(w):
    w = w.reshape(MLA_Q_RANK, MLA_H, MLA_QK)
    return jnp.concatenate([w[:, :, :MLA_NOPE].reshape(MLA_Q_RANK, -1),
                            w[:, :, MLA_NOPE:].reshape(MLA_Q_RANK, -1)], axis=1).astype(BF16)


def _pack_w_ukv(w):
    w = w.reshape(MLA_KV_RANK, MLA_H, MLA_NOPE + MLA_V)
    return jnp.concatenate([w[:, :, :MLA_NOPE].reshape(MLA_KV_RANK, -1),
                            w[:, :, MLA_NOPE:].reshape(MLA_KV_RANK, -1)], axis=1).astype(BF16)


def kernel(x_prompt, x_sample, state_gla, cache_mla, cache_diff_k, cache_diff_v, c, c_ctx,
           ada_w, ada_b, norm1_g, norm2_g, w_in, w_out, gla_gk_w, gla_gk_b, gla_onorm_g,
           mla_qnorm_g, mla_w_uq, mla_kvnorm_g, mla_w_ukv, mla_q_g, mla_k_g,
           diff_q_g, diff_k_g, diff_lambda, diff_onorm_g, pool_w, pool_scale,
           peer_wq, peer_keys, peer_u, peer_v):
    B, S, _ = x_prompt.shape
    DB, DS, _ = x_sample.shape
    L = ada_w.shape[0]
    t_ctx = B * S
    t_dec = DB * DS
    assert t_ctx % ROW_BLOCK == 0 and DS % ROW_BLOCK == 0 and t_ctx % DS == 0 and DB + 1 <= 16
    row = _mod_row_fn(t_ctx // ROW_BLOCK, DS // ROW_BLOCK)

    cond = jnp.concatenate([c_ctx[None, :], c, jnp.zeros((15 - DB, D), F32)], axis=0)
    mod = _ada_mod(cond, ada_w, ada_b)
    tables = _rope_tables(DS)
    kc_all = cache_diff_k.reshape(DB, L, -1, DIFF_H * LANES)
    vc_all = cache_diff_v.reshape(DB, L, -1, DIFF_H * LANES)

    state_gla_t = jnp.swapaxes(state_gla, -1, -2)
    mods = [mod[l].reshape(16, 1, 6 * D) for l in range(L)]
    x, h = _normmod_join(x_prompt.reshape(t_ctx, D), x_sample.reshape(t_dec, D),
                         norm1_g[0], mods[0], row)
    dec_blk0 = t_ctx // DS
    st_gla = jnp.zeros((B, L, 2, GLA_H, GLA_DV, GLA_DK), F32)
    st_mla = jnp.zeros((B, L, S, MLA_KV_RANK + MLA_ROPE), F32)
    st_dk = jnp.zeros((B, L, S, DIFF_H * LANES), F32)
    st_dv = jnp.zeros((B, L, S, DIFF_H * LANES), F32)
    for l in range(L):
        mod3 = mods[l]
        w_gla, w_mla, w_diff, w_pool = _pack_w_in(w_in[l])
        p_gla = _matmul(h, w_gla, name="in_gla")
        p_mla = _matmul(h, w_mla, name="in_mla")
        p_diff = _matmul(h, w_diff, name="in_diff")
        p_pool = _matmul(h, w_pool, name="in_pool")

        w_uq = _pack_w_uq(mla_w_uq[l])
        w_ukv = _pack_w_ukv(mla_w_ukv[l])
        gla_w = (gla_gk_w[l], gla_gk_b[l], gla_onorm_g[l])
        mla_w = (mla_qnorm_g[l], w_uq, mla_kvnorm_g[l], w_ukv, mla_q_g[l], mla_k_g[l])
        diff_w = (diff_q_g[l], diff_k_g[l], diff_lambda[l], diff_onorm_g[l])
        pool_wl = pool_w[l].astype(BF16)
        og_c, st_gla = _gla(p_gla, *gla_w, None, nb=B, S=S, blk0=0,
                            st_prev=st_gla, lidx=l, n_layers=L)
        og_d, _ = _gla(p_gla, *gla_w, state_gla_t, nb=DB, S=DS, blk0=dec_blk0,
                       st_prev=None, lidx=l, n_layers=L)
        om_c, st_mla = _mla(p_mla, *mla_w, None, None, nb=B, S=S, blk0=0, lidx=l,
                            new_prev=st_mla, n_layers=L)
        om_d, _ = _mla(p_mla, *mla_w, cache_mla, tables, nb=DB, S=DS, blk0=dec_blk0, lidx=l,
                       new_prev=None, n_layers=L)
        od_c, st_dk, st_dv = _diff(p_diff, *diff_w, None, None, None, nb=B, S=S, blk0=0, lidx=l,
                                   k_prev=st_dk, v_prev=st_dv, n_layers=L)
        od_d, _, _ = _diff(p_diff, *diff_w, kc_all, vc_all, tables, nb=DB, S=DS, blk0=dec_blk0,
                           lidx=l, k_prev=None, v_prev=None, n_layers=L)
        op_c = _pool(p_pool, pool_wl, pool_scale[l], nb=B, S=S, blk0=0)
        op_d = _pool(p_pool, pool_wl, pool_scale[l], nb=DB, S=DS, blk0=dec_blk0)

        x1, h2 = _outproj((og_c, om_c, od_c, op_c), (og_d, om_d, od_d, op_d),
                          w_out[l].astype(BF16), x, mod3, norm2_g[l], row)
        ii, jj, gg = _peer_topk(h2, peer_wq[l].astype(BF16), peer_keys[l].astype(BF16))
        ut = peer_u[l].reshape(PEER_EXPERTS // PEER_ET, PEER_ET, D).transpose(0, 2, 1).astype(BF16)
        peer = _peer_expert(h2, ii, jj, gg, ut, peer_v[l].astype(BF16))
        if l + 1 < L:
            x, h = _resid(x1, peer, mod3, row, (norm1_g[l + 1], mods[l + 1]))

    nc = t_ctx // ROW_BLOCK
    y_prompt, _ = _resid(x1, peer, mod3, row, blk0=0, nblk=nc)
    y_sample, _ = _resid(x1, peer, mod3, row, blk0=nc, nblk=t_dec // ROW_BLOCK)
    return (y_prompt.reshape(B, S, D), y_sample.reshape(DB, DS, D), jnp.swapaxes(st_gla, -1, -2), st_mla,
            st_dk.reshape(B, L, S, DIFF_H, 2 * DIFF_DQK), st_dv.reshape(B, L, S, DIFF_H, DIFF_DV))
```

```python
import functools
import math

import jax
import jax.numpy as jnp
import numpy as np
from jax import lax
from jax.experimental import pallas as pl
from jax.experimental.pallas import tpu as pltpu

F32 = jnp.float32
BF16 = jnp.bfloat16
HI = lax.Precision.HIGHEST

D = 2048
DEPTH = 2
GRID_W = 64
GROUP_W = 512
GLA_H, GLA_DK, GLA_DV, GLA_LR, GLA_NORM, GLA_CHUNK = 4, 64, 128, 16, 16.0, 64
MLA_H, MLA_NOPE, MLA_ROPE, MLA_V, MLA_QK = 4, 128, 64, 128, 192
MLA_Q_RANK, MLA_KV_RANK = 384, 128
DIFF_H, DIFF_DV, DIFF_DQK = 4, 128, 64
POOL_WINDOWS = (2, 4, 8, 16)
POOL_CH = 128
PEER_H, PEER_NKEYS, PEER_TOPK, PEER_DHALF = 8, 128, 16, 128
PEER_EXPERTS = PEER_NKEYS * PEER_NKEYS
ROPE_BASE = 10000.0
EPS = 1e-6

LANES = 128
VMEM_LIMIT = 56 * 1024 * 1024

GLA_COLS = 1664
MLA_COLS = 640
DIFF_COLS = 1536
POOL_COLS = 512

ROW_BLOCK = 256
GLA_NSEQ = 1
PEER_TB = 512
PEER_ET = 1024
PEER_SPLIT = 1
G_HALF = 64
G_PITCH = 72


def _cparams(sem):
    return pltpu.CompilerParams(dimension_semantics=sem, vmem_limit_bytes=VMEM_LIMIT)


def _dot(a, b, precision=None):
    return jnp.dot(a, b, preferred_element_type=F32, precision=precision)


def _dot_nt(a, b, precision=None):
    return lax.dot_general(a, b, (((1,), (1,)), ((), ())), preferred_element_type=F32,
                           precision=precision)


def _dot_tn(a, b, precision=None):
    return lax.dot_general(a, b, (((0,), (0,)), ((), ())), preferred_element_type=F32,
                           precision=precision)


def _silu(x):
    return x / (1.0 + jnp.exp(-x))


def _ada_kernel(c_ref, w_ref, b_ref, o_ref):
    a = _silu(c_ref[...]).astype(BF16)
    o_ref[0] = _dot(a, w_ref[0].astype(BF16)) + b_ref[0]


def _ada_mod(cond16, ada_w, ada_b):
    L, _, N = ada_w.shape
    tn = 1024
    return pl.pallas_call(
        _ada_kernel,
        out_shape=jax.ShapeDtypeStruct((L, 16, N), F32),
        grid=(L, N // tn),
        in_specs=[pl.BlockSpec((16, D), lambda l, j: (0, 0)),
                  pl.BlockSpec((1, D, tn), lambda l, j: (l, 0, j)),
                  pl.BlockSpec((1, 1, tn), lambda l, j: (l, 0, j))],
        out_specs=pl.BlockSpec((1, 16, tn), lambda l, j: (l, 0, j)),
        compiler_params=_cparams(("arbitrary", "arbitrary")),
        name="ada_mod",
    )(cond16, ada_w, ada_b.reshape(L, 1, N))


def _mod_row_fn(n_ctx_blocks, blocks_per_dec):
    def row(i):
        return jnp.where(i < n_ctx_blocks, 0, 1 + (i - n_ctx_blocks) // blocks_per_dec)
    return row


def _mod_spec(row, chunk):
    return pl.BlockSpec((1, 1, D), lambda i: (row(i), 0, chunk))


def _normmod_kernel(xa_ref, xb_ref, g_ref, sc_ref, sh_ref, xo_ref, o_ref, *, na):
    def emit(x_ref):
        x = x_ref[...]
        xo_ref[...] = x
        y = x * lax.rsqrt(jnp.mean(x * x, axis=-1, keepdims=True) + EPS) * g_ref[...]
        o_ref[...] = (y * (1.0 + sc_ref[0]) + sh_ref[0]).astype(BF16)

    @pl.when(pl.program_id(0) < na)
    def _():
        emit(xa_ref)

    @pl.when(pl.program_id(0) >= na)
    def _():
        emit(xb_ref)


def _normmod_join(xa, xb, g, mod3, row):
    na, nb = xa.shape[0] // ROW_BLOCK, xb.shape[0] // ROW_BLOCK
    T = xa.shape[0] + xb.shape[0]
    ospec = pl.BlockSpec((ROW_BLOCK, D), lambda i: (i, 0))
    return pl.pallas_call(
        functools.partial(_normmod_kernel, na=na),
        out_shape=[jax.ShapeDtypeStruct((T, D), F32), jax.ShapeDtypeStruct((T, D), BF16)],
        grid=(na + nb,),
        in_specs=[pl.BlockSpec((ROW_BLOCK, D), lambda i: (jnp.minimum(i, na - 1), 0)),
                  pl.BlockSpec((ROW_BLOCK, D), lambda i: (jnp.maximum(i - na, 0), 0)),
                  pl.BlockSpec((1, D), lambda i: (0, 0)), _mod_spec(row, 1), _mod_spec(row, 0)],
        out_specs=[ospec, ospec],
        compiler_params=_cparams(("parallel",)),
        name="normmod",
    )(xa, xb, g.reshape(1, D), mod3, mod3)


def _mm_kernel(h_ref, w_ref, o_ref):
    o_ref[...] = _dot(h_ref[...], w_ref[...]).astype(o_ref.dtype)


def _matmul(h, w, tm=512, out_dtype=F32, name="matmul"):
    T, K = h.shape
    N = w.shape[1]
    return pl.pallas_call(
        _mm_kernel,
        out_shape=jax.ShapeDtypeStruct((T, N), out_dtype),
        grid=(T // tm,),
        in_specs=[pl.BlockSpec((tm, K), lambda i: (i, 0)),
                  pl.BlockSpec((K, N), lambda i: (0, 0))],
        out_specs=pl.BlockSpec((tm, N), lambda i: (i, 0)),
        compiler_params=_cparams(("parallel",)),
        name=name,
    )(h, w)


def _gla_kernel(*refs, S, has_state, emit_state, unroll):
    if has_state:
        p_ref, gkw_ref, gkb_ref, on_ref, s0_ref = refs[:5]
        rest = refs[5:]
    else:
        p_ref, gkw_ref, gkb_ref, on_ref = refs[:4]
        s0_ref = None
        rest = refs[4:]
    if emit_state:
        o_ref, st_ref, lg_scr, of_scr, st_scr = rest
    else:
        o_ref, lg_scr, of_scr, st_scr = rest
        st_ref = None
    C = GLA_CHUNK
    n = S // C
    H, DK, DV = GLA_H, GLA_DK, GLA_DV
    HK = H * DK
    r_i = lax.broadcasted_iota(jnp.int32, (C, C), 0)
    c_i = lax.broadcasted_iota(jnp.int32, (C, C), 1)
    on_g = on_ref[...]

    for d in (0, 1):
        lr = p_ref[:, 1536 + GLA_LR * d:1536 + GLA_LR * (d + 1)]
        x = _dot(lr, gkw_ref[d], HI) + gkb_ref[d]
        lg_scr[...] = (jnp.minimum(x, 0.0) - jnp.log(1.0 + jnp.exp(-jnp.abs(x)))) * (1.0 / GLA_NORM)
        if d == 0:
            causal = r_i >= c_i
        else:
            causal = r_i <= c_i
        cmat = jnp.where(causal, 1.0, 0.0).astype(BF16)
        for s in range(GLA_NSEQ):
            for h in range(H):
                if has_state:
                    st_scr[s * H + h] = s0_ref[s, 0, d, h]
                else:
                    st_scr[s * H + h] = jnp.zeros((DV, DK), F32)

        def chunk(ci, carry, d=d, cmat=cmat, causal=causal):
            c = ci if d == 0 else n - 1 - ci
            for s in range(GLA_NSEQ):
                r0 = pl.multiple_of(s * S + c * C, C)
                gc = lg_scr[pl.ds(r0, C), :]
                g1 = gc.astype(BF16)
                rem = gc - g1.astype(F32)
                g2 = rem.astype(BF16)
                g3 = (rem - g2.astype(F32)).astype(BF16)
                cs = _dot(cmat, jnp.concatenate([g1, g2, g3], axis=1))
                cum = cs[:, :HK] + cs[:, HK:2 * HK] + cs[:, 2 * HK:]
                q = p_ref[pl.ds(r0, C), 0:256] * (DK ** -0.5)
                k = p_ref[pl.ds(r0, C), 256:512]
                qe = (q * jnp.exp(cum)).astype(BF16)
                ke = (k * jnp.exp(-cum)).astype(BF16)
                tot = jnp.sum(gc, axis=0, keepdims=True)
                kd = (k * jnp.exp(tot - cum)).astype(BF16)
                dec = jnp.exp(tot)
                outs = []
                for h in range(H):
                    sl = slice(h * DK, (h + 1) * DK)
                    v_h = p_ref[pl.ds(r0, C), 512 + h * DV:512 + (h + 1) * DV].astype(BF16)
                    st = st_scr[s * H + h]
                    att = _dot_nt(qe[:, sl], ke[:, sl])
                    att = jnp.where(causal, att, 0.0).astype(BF16)
                    o_h = _dot_nt(qe[:, sl], st.astype(BF16)) + _dot(att, v_h)
                    st_scr[s * H + h] = dec[:, sl] * st + _dot_tn(v_h, kd[:, sl])
                    outs.append(o_h)
                o_c = jnp.concatenate(outs, axis=1)
                if d == 0:
                    of_scr[pl.ds(r0, C), :] = o_c
                else:
                    o_c = o_c + of_scr[pl.ds(r0, C), :]
                    gate = p_ref[pl.ds(r0, C), 1024:1536]
                    res = []
                    for h in range(H):
                        oh = o_c[:, h * DV:(h + 1) * DV]
                        oh = oh * lax.rsqrt(jnp.mean(oh * oh, axis=-1, keepdims=True) + EPS) * on_g
                        res.append(oh * _silu(gate[:, h * DV:(h + 1) * DV]))
                    o_ref[pl.ds(r0, C), :] = jnp.concatenate(res, axis=1).astype(o_ref.dtype)
            return carry

        lax.fori_loop(0, n, chunk, 0, unroll=unroll)
        if emit_state:
            for s in range(GLA_NSEQ):
                for h in range(H):
                    st_ref[s, 0, d, h] = st_scr[s * H + h]


def _carried_call(kern, *, grid, in_specs, args, outs, scratch_shapes, name):
    n_in = len(args)
    carried = [(k, o[2]) for k, o in enumerate(outs) if o[2] is not None]
    n_carried = len(carried)

    def body(*refs):
        kern(*refs[:n_in], *refs[n_in + n_carried:])

    return pl.pallas_call(
        body,
        out_shape=[o[0] for o in outs], grid=grid,
        in_specs=list(in_specs) + [pl.BlockSpec(memory_space=pl.ANY)] * n_carried,
        out_specs=[o[1] for o in outs],
        scratch_shapes=scratch_shapes,
        input_output_aliases={n_in + j: k for j, (k, _) in enumerate(carried)},
        compiler_params=_cparams(("parallel",)),
        name=name,
    )(*args, *[prev for _, prev in carried])


def _gla(proj, gk_w, gk_b, onorm_g, s0, *, nb, S, blk0, st_prev, lidx, n_layers):
    has_state = s0 is not None
    emit_state = not has_state
    ns = GLA_NSEQ
    assert nb % ns == 0 and blk0 % ns == 0
    gb0 = blk0 // ns
    rows = ns * S
    in_specs = [pl.BlockSpec((rows, GLA_COLS), lambda b: (gb0 + b, 0)),
                pl.BlockSpec((2, GLA_LR, GLA_H * GLA_DK), lambda b: (0, 0, 0)),
                pl.BlockSpec((2, 1, GLA_H * GLA_DK), lambda b: (0, 0, 0)),
                pl.BlockSpec((1, GLA_DV), lambda b: (0, 0))]
    args = [proj, gk_w, gk_b.reshape(2, 1, -1), onorm_g.reshape(1, -1)]
    st_dims = (2, GLA_H, GLA_DV, GLA_DK)
    if has_state:
        in_specs.append(pl.BlockSpec((ns, 1) + st_dims, lambda b: (b, lidx, 0, 0, 0, 0)))
        args.append(s0)
    outs = [(jax.ShapeDtypeStruct((nb * S, GLA_H * GLA_DV), BF16),
             pl.BlockSpec((rows, GLA_H * GLA_DV), lambda b: (b, 0)), None)]
    if emit_state:
        outs.append((jax.ShapeDtypeStruct((nb, n_layers) + st_dims, F32),
                     pl.BlockSpec((ns, 1) + st_dims, lambda b: (b, lidx, 0, 0, 0, 0)), st_prev))
    res = _carried_call(
        functools.partial(_gla_kernel, S=S, has_state=has_state, emit_state=emit_state,
                          unroll=4 // ns),
        grid=(nb // ns,), in_specs=in_specs, args=args, outs=outs,
        scratch_shapes=[pltpu.VMEM((rows, GLA_H * GLA_DK), F32),
                        pltpu.VMEM((rows, GLA_H * GLA_DV), F32),
                        pltpu.VMEM((ns * GLA_H, GLA_DV, GLA_DK), F32)],
        name="gla_dec" if has_state else "gla_ctx")
    return res if emit_state else (res[0], None)


def _rope_tables(n_tokens):
    t = jnp.arange(n_tokens)
    row = (t // GRID_W).astype(F32)
    col = (t % GRID_W).astype(F32)
    half = 16
    freq = ROPE_BASE ** (-jnp.arange(half, dtype=F32) / half)
    ar = row[:, None] * freq[None, :]
    ac = col[:, None] * freq[None, :]
    cos = jnp.concatenate([jnp.cos(ar), jnp.cos(ar), jnp.cos(ac), jnp.cos(ac)], axis=1)
    sin = jnp.concatenate([-jnp.sin(ar), jnp.sin(ar), -jnp.sin(ac), jnp.sin(ac)], axis=1)
    return jnp.tile(cos, (1, 2)), jnp.tile(sin, (1, 2))


def _rope128(x, cos, sin):
    lane = lax.broadcasted_iota(jnp.int32, x.shape, 1)
    first = (lane % 32) < 16
    partner = jnp.where(first, pltpu.roll(x, LANES - 16, 1), pltpu.roll(x, 16, 1))
    return x * cos + partner * sin


def _mla_kernel(*refs, S, P, rope, qb):
    if P:
        (p_ref, qn_ref, wuq_ref, kvn_ref, wukv_ref, qg_ref, kg_ref, cache_ref, cos_ref, sin_ref,
         o_ref, k_scr, v_scr) = refs
        new_ref = None
    else:
        (p_ref, qn_ref, wuq_ref, kvn_ref, wukv_ref, qg_ref, kg_ref,
         o_ref, new_ref, k_scr, v_scr) = refs
    H = MLA_H
    NK = P + S
    qg = qg_ref[...]
    kg = kg_ref[...]
    qg_n, qg_r = qg[:, :MLA_NOPE], qg[:, MLA_NOPE:]
    kg_n, kg_r = kg[:, :MLA_NOPE], kg[:, MLA_NOPE:]
    qg_r2 = jnp.concatenate([qg_r, qg_r], axis=1)
    kg_r2 = jnp.concatenate([kg_r, kg_r], axis=1)

    ckv = p_ref[:, 384:512]
    ckv = ckv * lax.rsqrt(jnp.mean(ckv * ckv, axis=-1, keepdims=True) + EPS) * kvn_ref[...]
    kpe = p_ref[:, 512:576]
    if new_ref is not None:
        new_ref[0, 0, :, 0:MLA_KV_RANK] = ckv
        new_ref[0, 0, :, MLA_KV_RANK:] = kpe

    def expand(c, pe, do_rope, r0, nrows):
        kv = _dot(c.astype(BF16), wukv_ref[...])
        pe2 = jnp.concatenate([pe, pe], axis=1)
        pe_ss = jnp.sum(pe * pe, axis=-1, keepdims=True)
        pe_g = pe2 * kg_r2
        if do_rope:
            pe_g = _rope128(pe_g, cos_ref[...], sin_ref[...])
        for h in range(H):
            kn = kv[:, h * MLA_NOPE:(h + 1) * MLA_NOPE]
            r = lax.rsqrt((jnp.sum(kn * kn, axis=-1, keepdims=True) + pe_ss) / MLA_QK + EPS)
            kfull = jnp.concatenate([kn * kg_n * r, pe_g[:, :MLA_ROPE] * r], axis=1)
            k_scr[h, r0:r0 + nrows, :] = kfull.astype(BF16)
            v_scr[h, r0:r0 + nrows, :] = jnp.concatenate(
                [kv[:, 512 + h * MLA_V:512 + (h + 1) * MLA_V].astype(BF16),
                 jnp.ones((nrows, MLA_V), BF16)], axis=1)

    if P:
        cc = cache_ref[0, 0]
        expand(cc[:, :MLA_KV_RANK], cc[:, MLA_KV_RANK:], False, 0, P)
    expand(ckv, kpe, rope, P, S)

    cq = p_ref[:, 0:384]
    cq = cq * lax.rsqrt(jnp.mean(cq * cq, axis=-1, keepdims=True) + EPS) * qn_ref[...]
    qall = _dot(cq.astype(BF16), wuq_ref[...])
    scale = MLA_QK ** -0.5
    lane = lax.broadcasted_iota(jnp.int32, (S, LANES), 1)
    for hp in range(H // 2):
        rs = []
        qns = []
        for j in range(2):
            h = 2 * hp + j
            qn = qall[:, h * MLA_NOPE:(h + 1) * MLA_NOPE]
            qr = qall[:, 512 + h * MLA_ROPE:512 + (h + 1) * MLA_ROPE]
            ss = jnp.sum(qn * qn, axis=-1, keepdims=True) + jnp.sum(qr * qr, axis=-1, keepdims=True)
            r = lax.rsqrt(ss / MLA_QK + EPS)
            rs.append(r)
            qns.append(qn * qg_n * r)
        qr2 = qall[:, 512 + hp * LANES:512 + (hp + 1) * LANES] * qg_r2
        qr2 = qr2 * jnp.where(lane < MLA_ROPE, rs[0], rs[1])
        if rope:
            qr2 = _rope128(qr2, cos_ref[...], sin_ref[...])
        for j in range(2):
            h = 2 * hp + j
            qfull = jnp.concatenate([qns[j], qr2[:, j * MLA_ROPE:(j + 1) * MLA_ROPE]], axis=1)
            qfull = (qfull * scale).astype(BF16)
            kh = k_scr[h]
            vh = v_scr[h]
            for i in range(S // qb):
                sc = _dot_nt(qfull[i * qb:(i + 1) * qb], kh)
                m = jnp.max(sc, axis=-1, keepdims=True)
                e = jnp.exp(sc - m)
                ov = _dot(e.astype(BF16), vh)
                o = ov[:, :MLA_V] / ov[:, MLA_V:]
                o_ref[i * qb:(i + 1) * qb, h * MLA_V:(h + 1) * MLA_V] = o.astype(o_ref.dtype)


def _mla(proj, qnorm_g, w_uq, kvnorm_g, w_ukv, q_g, k_g, cache, tables, *, nb, S, blk0, lidx,
         new_prev, n_layers):
    P = 0 if cache is None else cache.shape[2]
    const2 = lambda b: (0, 0)
    in_specs = [pl.BlockSpec((S, MLA_COLS), lambda b: (blk0 + b, 0)),
                pl.BlockSpec((1, MLA_Q_RANK), const2),
                pl.BlockSpec((MLA_Q_RANK, MLA_H * MLA_QK), const2),
                pl.BlockSpec((1, MLA_KV_RANK), const2),
                pl.BlockSpec((MLA_KV_RANK, MLA_H * (MLA_NOPE + MLA_V)), const2),
                pl.BlockSpec((1, MLA_QK), const2),
                pl.BlockSpec((1, MLA_QK), const2)]
    args = [proj, qnorm_g.reshape(1, -1), w_uq, kvnorm_g.reshape(1, -1), w_ukv,
            q_g.reshape(1, -1), k_g.reshape(1, -1)]
    if P:
        in_specs += [pl.BlockSpec((1, 1, P, MLA_KV_RANK + MLA_ROPE), lambda b: (b, lidx, 0, 0)),
                     pl.BlockSpec((S, LANES), const2), pl.BlockSpec((S, LANES), const2)]
        args += [cache, tables[0], tables[1]]
    outs = [(jax.ShapeDtypeStruct((nb * S, MLA_H * MLA_V), BF16),
             pl.BlockSpec((S, MLA_H * MLA_V), lambda b: (b, 0)), None)]
    if not P:
        cw = MLA_KV_RANK + MLA_ROPE
        outs.append((jax.ShapeDtypeStruct((nb, n_layers, S, cw), F32),
                     pl.BlockSpec((1, 1, S, cw), lambda b: (b, lidx, 0, 0)), new_prev))
    res = _carried_call(
        functools.partial(_mla_kernel, S=S, P=P, rope=bool(P), qb=min(S, 256)),
        grid=(nb,), in_specs=in_specs, args=args, outs=outs,
        scratch_shapes=[pltpu.VMEM((MLA_H, P + S, MLA_QK), BF16),
                        pltpu.VMEM((MLA_H, P + S, 2 * MLA_V), BF16)],
        name="mla_dec" if P else "mla_ctx")
    return (res[0], None) if P else res


def _group_rms64(x, g2):
    lane = lax.broadcasted_iota(jnp.int32, x.shape, 1)
    lo = lane < DIFF_DQK
    xx = x * x
    s_all = jnp.sum(xx, axis=-1, keepdims=True)
    s_lo = jnp.sum(jnp.where(lo, xx, 0.0), axis=-1, keepdims=True)
    ms = jnp.where(lo, s_lo, s_all - s_lo) * (1.0 / DIFF_DQK)
    return x * lax.rsqrt(ms + EPS) * g2


def _diff_kernel(*refs, S, P, rope, qb, lam_init):
    if P:
        (p_ref, qg_ref, kg_ref, lam_ref, on_ref, kc_ref, vc_ref, cos_ref, sin_ref,
         o_ref, k_scr, v_scr) = refs
        ko_ref = vo_ref = None
    else:
        (p_ref, qg_ref, kg_ref, lam_ref, on_ref, o_ref, ko_ref, vo_ref, k_scr, v_scr) = refs
    H = DIFF_H
    qg2 = jnp.concatenate([qg_ref[...], qg_ref[...]], axis=1)
    kg2 = jnp.concatenate([kg_ref[...], kg_ref[...]], axis=1)
    lp = lam_ref[...]
    l1 = jnp.sum(jnp.sum(lp[0:1] * lp[1:2], axis=-1, keepdims=True), axis=0, keepdims=True)
    l2 = jnp.sum(jnp.sum(lp[2:3] * lp[3:4], axis=-1, keepdims=True), axis=0, keepdims=True)
    lam = jnp.exp(l1) - jnp.exp(l2) + lam_init
    scale = DIFF_DQK ** -0.5
    lane = lax.broadcasted_iota(jnp.int32, (S, LANES), 1)
    lo = lane < DIFF_DQK
    on_g = on_ref[...]

    if vo_ref is not None:
        vo_ref[0, 0] = p_ref[:, 1024:1536]
    for h in range(H):
        sl = slice(h * LANES, (h + 1) * LANES)
        kn = _group_rms64(p_ref[:, 512 + h * LANES:512 + (h + 1) * LANES], kg2)
        if ko_ref is not None:
            ko_ref[0, 0, :, sl] = kn
        if rope:
            kn = _rope128(kn, cos_ref[...], sin_ref[...])
        if P:
            k_scr[0:P, :] = kc_ref[0, 0, :, sl].astype(BF16)
            v_scr[0:P, :] = jnp.concatenate(
                [vc_ref[0, 0, :, sl].astype(BF16), jnp.ones((P, DIFF_DV), BF16)], axis=1)
        k_scr[P:P + S, :] = kn.astype(BF16)
        v_scr[P:P + S, :] = jnp.concatenate(
            [p_ref[:, 1024 + h * LANES:1024 + (h + 1) * LANES].astype(BF16),
             jnp.ones((S, DIFF_DV), BF16)], axis=1)
        qn = _group_rms64(p_ref[:, sl], qg2)
        if rope:
            qn = _rope128(qn, cos_ref[...], sin_ref[...])
        qn = qn * scale
        q0 = jnp.where(lo, qn, 0.0).astype(BF16)
        q1 = jnp.where(lo, 0.0, qn).astype(BF16)
        kh = k_scr[...]
        vh = v_scr[...]
        for i in range(S // qb):
            rows = slice(i * qb, (i + 1) * qb)
            os = []
            for qq in (q0, q1):
                sc = _dot_nt(qq[rows], kh)
                m = jnp.max(sc, axis=-1, keepdims=True)
                e = jnp.exp(sc - m)
                ov = _dot(e.astype(BF16), vh)
                os.append(ov[:, :DIFF_DV] / ov[:, DIFF_DV:])
            o = os[0] - lam * os[1]
            o = o * lax.rsqrt(jnp.mean(o * o, axis=-1, keepdims=True) + EPS) * on_g
            o_ref[rows, sl] = (o * (1.0 - lam_init)).astype(o_ref.dtype)


def _diff(proj, q_g, k_g, lam_p, onorm_g, kc, vc, tables, *, nb, S, blk0, lidx,
          k_prev, v_prev, n_layers):
    P = 0 if kc is None else kc.shape[2]
    const2 = lambda b: (0, 0)
    in_specs = [pl.BlockSpec((S, DIFF_COLS), lambda b: (blk0 + b, 0)),
                pl.BlockSpec((1, DIFF_DQK), const2), pl.BlockSpec((1, DIFF_DQK), const2),
                pl.BlockSpec((4, DIFF_DQK), const2), pl.BlockSpec((1, DIFF_DV), const2)]
    args = [proj, q_g.reshape(1, -1), k_g.reshape(1, -1), lam_p, onorm_g.reshape(1, -1)]
    if P:
        cspec = pl.BlockSpec((1, 1, P, DIFF_H * LANES), lambda b: (b, lidx, 0, 0))
        in_specs += [cspec, cspec, pl.BlockSpec((S, LANES), const2), pl.BlockSpec((S, LANES), const2)]
        args += [kc, vc, tables[0], tables[1]]
    lam_init = 0.8 - 0.6 * math.exp(-0.3 * lidx)
    outs = [(jax.ShapeDtypeStruct((nb * S, DIFF_H * DIFF_DV), BF16),
             pl.BlockSpec((S, DIFF_H * DIFF_DV), lambda b: (b, 0)), None)]
    if not P:
        cshape = jax.ShapeDtypeStruct((nb, n_layers, S, DIFF_H * LANES), F32)
        cspec = pl.BlockSpec((1, 1, S, DIFF_H * LANES), lambda b: (b, lidx, 0, 0))
        outs += [(cshape, cspec, k_prev), (cshape, cspec, v_prev)]
    res = _carried_call(
        functools.partial(_diff_kernel, S=S, P=P, rope=bool(P), qb=min(S, 256), lam_init=lam_init),
        grid=(nb,), in_specs=in_specs, args=args, outs=outs,
        scratch_shapes=[pltpu.VMEM((P + S, LANES), BF16), pltpu.VMEM((P + S, 2 * DIFF_DV), BF16)],
        name="diff_dec" if P else "diff_ctx")
    return (res[0], None, None) if P else res


POOL_PAD = 8


def _pool_kernel(p_ref, w_ref, sc_ref, o_ref, pad_scr, *, S):
    zeros = jnp.zeros((POOL_PAD, GROUP_W), F32)
    pad_scr[0:POOL_PAD, :] = zeros
    pad_scr[POOL_PAD + S:POOL_PAD + S + POOL_PAD, :] = zeros
    pad_scr[POOL_PAD:POOL_PAD + S, :] = p_ref[...]
    t = lax.broadcasted_iota(jnp.int32, (S, POOL_CH), 0)
    for gi, win in enumerate(POOL_WINDOWS):
        cs = slice(gi * POOL_CH, (gi + 1) * POOL_CH)
        acc = None
        for dlt in range(-(win // 2), win // 2):
            piece = pad_scr[POOL_PAD + dlt:POOL_PAD + dlt + S, cs]
            acc = piece if acc is None else acc + piece
        cnt = (jnp.minimum(t + win // 2, S) - jnp.maximum(t - win // 2, 0)).astype(F32)
        y = (acc / cnt - p_ref[:, cs]).astype(BF16)
        o_ref[:, cs] = (_dot(y, w_ref[gi]) * sc_ref[:, cs]).astype(o_ref.dtype)


def _pool(proj, w, scale, *, nb, S, blk0):
    return _carried_call(
        functools.partial(_pool_kernel, S=S),
        grid=(nb,),
        in_specs=[pl.BlockSpec((S, POOL_COLS), lambda b: (blk0 + b, 0)),
                  pl.BlockSpec((len(POOL_WINDOWS), POOL_CH, POOL_CH), lambda b: (0, 0, 0)),
                  pl.BlockSpec((1, GROUP_W), lambda b: (0, 0))],
        args=[proj, w, scale.reshape(1, -1)],
        outs=[(jax.ShapeDtypeStruct((nb * S, GROUP_W), BF16),
               pl.BlockSpec((S, GROUP_W), lambda b: (b, 0)), None)],
        scratch_shapes=[pltpu.VMEM((S + 2 * POOL_PAD, GROUP_W), F32)],
        name="pool")[0]


def _outproj_kernel(*refs, na):
    ctx_refs, dec_refs = refs[0:4], refs[4:8]
    w_ref, x_ref, ga_ref, g2_ref, sc_ref, sh_ref, x1_ref, h2_ref = refs[8:]
    is_ctx = pl.program_id(0) < na
    mix = None
    for j in range(4):
        o_j = jnp.where(is_ctx, ctx_refs[j][...], dec_refs[j][...])
        part = _dot(o_j, w_ref[j * GROUP_W:(j + 1) * GROUP_W, :])
        mix = part if mix is None else mix + part
    x1 = x_ref[...] + ga_ref[0] * mix
    x1_ref[...] = x1
    y = x1 * lax.rsqrt(jnp.mean(x1 * x1, axis=-1, keepdims=True) + EPS) * g2_ref[...]
    h2_ref[...] = (y * (1.0 + sc_ref[0]) + sh_ref[0]).astype(BF16)


def _outproj(o_ctx, o_dec, w_out, x, mod3, norm2_g, row):
    T = x.shape[0]
    na = o_ctx[0].shape[0] // ROW_BLOCK
    cspec = pl.BlockSpec((ROW_BLOCK, GROUP_W), lambda i: (jnp.minimum(i, na - 1), 0))
    dspec = pl.BlockSpec((ROW_BLOCK, GROUP_W), lambda i: (jnp.maximum(i - na, 0), 0))
    xspec = pl.BlockSpec((ROW_BLOCK, D), lambda i: (i, 0))
    return pl.pallas_call(
        functools.partial(_outproj_kernel, na=na),
        out_shape=[jax.ShapeDtypeStruct((T, D), F32), jax.ShapeDtypeStruct((T, D), BF16)],
        grid=(T // ROW_BLOCK,),
        in_specs=[cspec] * 4 + [dspec] * 4 + [
                  pl.BlockSpec((D, D), lambda i: (0, 0)),
                  xspec, _mod_spec(row, 2),
                  pl.BlockSpec((1, D), lambda i: (0, 0)),
                  _mod_spec(row, 4), _mod_spec(row, 3)],
        out_specs=[xspec, xspec],
        compiler_params=_cparams(("parallel",)),
        name="outproj",
    )(*o_ctx, *o_dec, w_out, x, mod3, norm2_g.reshape(1, D), mod3, mod3)


def _topk_chains(x_refs, nrows, tt):
    rows = lax.broadcasted_iota(jnp.int32, (nrows, tt), 0)
    krow = lax.broadcasted_iota(jnp.int32, (PEER_TOPK, tt), 0)
    ng = nrows // 8
    rows_f = rows.astype(F32)
    sub = lax.broadcasted_iota(jnp.int32, (8, tt), 0).astype(F32)
    row_groups = [sub + float(8 * g) for g in range(ng)]

    def argmax_rows(x):
        items = [(x[8 * g:8 * (g + 1)], row_groups[g]) for g in range(ng)]
        while len(items) > 1:
            nxt = []
            for j in range(0, len(items) - 1, 2):
                (a, ra), (b, rb) = items[j], items[j + 1]
                gt = b > a
                nxt.append((jnp.where(gt, b, a), jnp.where(gt, rb, ra)))
            if len(items) % 2:
                nxt.append(items[-1])
            items = nxt
        v, r = items[0]
        m = jnp.max(v, axis=0, keepdims=True)
        pos = jnp.min(jnp.where(v == m, r, float(nrows)), axis=0, keepdims=True)
        return m, pos

    def body(k, carry):
        out = []
        for x_ref, (vals, sel) in zip(x_refs, carry):
            x = x_ref[...]
            m, pos = argmax_rows(x)
            x_ref[...] = jnp.where(rows_f == pos, -jnp.inf, x)
            out.append((jnp.where(krow == k, m, vals), jnp.where(krow == k, pos, sel)))
        return tuple(out)

    init = (jnp.zeros((PEER_TOPK, tt), F32), jnp.zeros((PEER_TOPK, tt), F32))
    res = lax.fori_loop(0, PEER_TOPK, body, tuple(init for _ in x_refs))
    return [(vals, sel.astype(jnp.int32)) for vals, sel in res]


CAND_ROWS = 80


def _cand_ab(r):
    a = jnp.where(r < 16, 0, jnp.where(r < 72, 1 + ((r - 16) >> 3), r - 64))
    b = jnp.where(r < 16, r, jnp.where(r < 72, (r - 16) & 7, 0))
    return a, b


def _peer_topk_kernel(h_ref, wq_ref, keys_ref, i_ref, j_ref, g_ref,
                      q_scr, sc_scr, cand_scr, i_scr, j_scr, g_scr, *, tt):
    q = _dot(h_ref[...], wq_ref[...]).astype(BF16)
    for hp in range(2 * PEER_H):
        q_scr[hp] = q[:, hp * PEER_DHALF:(hp + 1) * PEER_DHALF]
    K = PEER_TOPK

    def head_pair(hh, carry):
        for u in range(2):
            for p in range(2):
                h = 2 * hh + u
                sc_scr[2 * u + p] = _dot_nt(keys_ref[h, p], q_scr[2 * h + p])
        tops = _topk_chains([sc_scr.at[c] for c in range(4)], PEER_NKEYS, tt)
        for u in range(2):
            (s0, _), (s1, _) = tops[2 * u], tops[2 * u + 1]
            cand_scr[u, 0:K, :] = s0[0:1, :] + s1
            for a in range(1, 8):
                cand_scr[u, 8 + 8 * a:16 + 8 * a, :] = s0[a:a + 1, :] + s1[0:8, :]
            cand_scr[u, 72:80, :] = s0[8:16, :] + s1[0:1, :]
        picks = _topk_chains([cand_scr.at[u] for u in range(2)], CAND_ROWS, tt)
        for u in range(2):
            (_, i0), (_, i1) = tops[2 * u], tops[2 * u + 1]
            best, pos = picks[u]
            a_sel, b_sel = _cand_ab(pos)
            i_sel = jnp.zeros((K, tt), jnp.int32)
            j_sel = jnp.zeros((K, tt), jnp.int32)
            for a in range(K):
                i_sel = jnp.where(a_sel == a, i0[a:a + 1, :], i_sel)
                j_sel = jnp.where(b_sel == a, i1[a:a + 1, :], j_sel)
            e = jnp.exp(best - best[0:1, :])
            i_scr[2 * hh + u] = i_sel.astype(F32)
            j_scr[2 * hh + u] = j_sel.astype(F32)
            g_scr[2 * hh + u] = e / jnp.sum(e, axis=0, keepdims=True)
        return carry

    lax.fori_loop(0, PEER_H // 2, head_pair, 0)
    nrow = PEER_H * K
    i_ref[...] = i_scr[...].reshape(nrow, tt).T.astype(jnp.int32)
    j_ref[...] = j_scr[...].reshape(nrow, tt).T.astype(jnp.int32)
    g_ref[...] = g_scr[...].reshape(nrow, tt).T


def _peer_topk(h2, wq, keys, tt=ROW_BLOCK):
    T = h2.shape[0]
    npick = PEER_H * PEER_TOPK
    ospec = pl.BlockSpec((tt, npick), lambda i: (i, 0))
    return pl.pallas_call(
        functools.partial(_peer_topk_kernel, tt=tt),
        out_shape=[jax.ShapeDtypeStruct((T, npick), jnp.int32),
                   jax.ShapeDtypeStruct((T, npick), jnp.int32),
                   jax.ShapeDtypeStruct((T, npick), F32)],
        grid=(T // tt,),
        in_specs=[pl.BlockSpec((tt, D), lambda i: (i, 0)),
                  pl.BlockSpec((D, D), lambda i: (0, 0)),
                  pl.BlockSpec((PEER_H, 2, PEER_NKEYS, PEER_DHALF), lambda i: (0, 0, 0, 0))],
        out_specs=[ospec, ospec, ospec],
        scratch_shapes=[pltpu.VMEM((2 * PEER_H, tt, PEER_DHALF), BF16),
                        pltpu.VMEM((4, PEER_NKEYS, tt), F32),
                        pltpu.VMEM((2, CAND_ROWS, tt), F32),
                        pltpu.VMEM((PEER_H, PEER_TOPK, tt), F32),
                        pltpu.VMEM((PEER_H, PEER_TOPK, tt), F32),
                        pltpu.VMEM((PEER_H, PEER_TOPK, tt), F32)],
        compiler_params=_cparams(("parallel",)),
        name="peer_topk",
    )(h2, wq, keys)


def _gelu_tanh(x):
    return 0.5 * x * (1.0 + jnp.tanh(0.7978845608028654 * (x + 0.044715 * x * x * x)))


def _peer_expert_kernel(h_ref, i_ref, j_ref, g_ref, ut_ref, v_ref, o_ref, gate_scr, *, tb, et):
    e = pl.program_id(1)
    nk = PEER_NKEYS
    npick = PEER_H * PEER_TOPK
    tiles = et // nk

    @pl.when(e == 0)
    def _():
        o_ref[...] = jnp.zeros_like(o_ref)
        sub = lax.broadcasted_iota(jnp.int32, (nk, npick), 0)
        zero = jnp.zeros((nk, npick), BF16)

        def pair(tp, carry):
            lhs, rhs = [], []
            for u in range(2):
                t = 2 * tp + u
                irow = i_ref[pl.ds(t, 1), :]
                jrow = j_ref[pl.ds(t, 1), :]
                grow = g_ref[pl.ds(t, 1), :]
                lhs.append(jnp.where(sub == irow, 1.0, 0.0).astype(BF16))
                xt = jnp.where(sub == jrow, grow, 0.0).astype(BF16)
                rhs.append(jnp.concatenate([xt, zero] if u == 0 else [zero, xt], axis=1))
            g2 = _dot_nt(jnp.concatenate(lhs, axis=1), jnp.concatenate(rhs, axis=0))
            for u in range(2):
                g = g2[:, u * nk:(u + 1) * nk].astype(BF16).astype(F32)
                hi = pltpu.bitcast(g[:G_HALF], jnp.uint32)
                lo = pltpu.bitcast(g[G_HALF:], jnp.uint32)
                r0 = pl.multiple_of((2 * tp + u) * G_PITCH, 8)
                gate_scr[pl.ds(r0, G_HALF), :] = hi | (lo >> 16)
            return carry

        lax.fori_loop(0, tb // 2, pair, 0, unroll=16)

    i0 = e * tiles
    upper = i0 < G_HALF
    r0 = jnp.where(upper, i0, i0 - G_HALF)
    rows = tb // PEER_SPLIT
    for part in range(PEER_SPLIT):
        rs = slice(part * rows, (part + 1) * rows)
        s = _dot(h_ref[rs, :], ut_ref[0])
        words = jnp.concatenate(
            [gate_scr[pl.ds(part * rows * G_PITCH + r0 + il, rows, stride=G_PITCH), :]
             for il in range(tiles)], axis=1)
        bits = jnp.where(upper, words & jnp.uint32(0xFFFF0000), words << 16)
        c = (pltpu.bitcast(bits, F32) * _gelu_tanh(s)).astype(BF16)
        o_ref[rs, :] += _dot(c, v_ref[...])


def _peer_expert(h2, ii, jj, gg, ut, v, tb=PEER_TB, et=PEER_ET):
    T = h2.shape[0]
    ne = PEER_EXPERTS // et
    assert G_HALF % (et // PEER_NKEYS) == 0
    npick = PEER_H * PEER_TOPK
    tspec = pl.BlockSpec((tb, npick), lambda i, e: (i, 0))
    xspec = pl.BlockSpec((tb, D), lambda i, e: (i, 0))
    return pl.pallas_call(
        functools.partial(_peer_expert_kernel, tb=tb, et=et),
        out_shape=jax.ShapeDtypeStruct((T, D), F32),
        grid=(T // tb, ne),
        in_specs=[xspec, tspec, tspec, tspec,
                  pl.BlockSpec((1, D, et), lambda i, e: (e, 0, 0)),
                  pl.BlockSpec((et, D), lambda i, e: (e, 0))],
        out_specs=xspec,
        scratch_shapes=[pltpu.VMEM((tb * G_PITCH, LANES), jnp.uint32)],
        compiler_params=_cparams(("parallel", "arbitrary")),
        name="peer_expert",
    )(h2, ii, jj, gg, ut, v)


def _resid_kernel(*refs, emit_h):
    if emit_h:
        x1_ref, p_ref, gf_ref, g_ref, sc_ref, sh_ref, x_ref, h_ref = refs
    else:
        x1_ref, p_ref, gf_ref, x_ref = refs
    x = x1_ref[...] + gf_ref[0] * p_ref[...]
    x_ref[...] = x
    if emit_h:
        y = x * lax.rsqrt(jnp.mean(x * x, axis=-1, keepdims=True) + EPS) * g_ref[...]
        h_ref[...] = (y * (1.0 + sc_ref[0]) + sh_ref[0]).astype(BF16)


def _resid(x1, peer, mod3, row, next_norm=None, blk0=0, nblk=None):
    nblk = x1.shape[0] // ROW_BLOCK if nblk is None else nblk
    ispec = pl.BlockSpec((ROW_BLOCK, D), lambda i: (blk0 + i, 0))
    ospec = pl.BlockSpec((ROW_BLOCK, D), lambda i: (i, 0))
    mspec = lambda chunk: pl.BlockSpec((1, 1, D), lambda i: (row(blk0 + i), 0, chunk))
    in_specs = [ispec, ispec, mspec(5)]
    args = [x1, peer, mod3]
    out_shape = [jax.ShapeDtypeStruct((nblk * ROW_BLOCK, D), F32)]
    out_specs = [ospec]
    if next_norm is not None:
        g, mod3n = next_norm
        in_specs += [pl.BlockSpec((1, D), lambda i: (0, 0)), mspec(1), mspec(0)]
        args += [g.reshape(1, D), mod3n, mod3n]
        out_shape.append(jax.ShapeDtypeStruct((nblk * ROW_BLOCK, D), BF16))
        out_specs.append(ospec)
    res = pl.pallas_call(
        functools.partial(_resid_kernel, emit_h=next_norm is not None),
        out_shape=out_shape, grid=(nblk,), in_specs=in_specs, out_specs=out_specs,
        compiler_params=_cparams(("parallel",)),
        name="resid",
    )(*args)
    return res if next_norm is not None else (res[0], None)


def _pack_w_in(w):
    o = np.cumsum([0, 256, 256, 512, 512, 32, 384, 128, 64, 512, 512, 512, 512]).tolist()
    z = lambda n: jnp.zeros((D, n), w.dtype)
    w_gla = jnp.concatenate([w[:, o[0]:o[5]], z(GLA_COLS - 1568)], axis=1)
    w_mla = jnp.concatenate([w[:, o[5]:o[8]], z(MLA_COLS - 576)], axis=1)
    w_diff = w[:, o[8]:o[11]]
    w_pool = w[:, o[11]:o[12]]
    return [a.astype(BF16) for a in (w_gla, w_mla, w_diff, w_pool)]


def _pack_w_uq(w):
    w = w.reshape(MLA_Q_RANK, MLA_H, MLA_QK)
    return jnp.concatenate([w[:, :, :MLA_NOPE].reshape(MLA_Q_RANK, -1),
                            w[:, :, MLA_NOPE:].reshape(MLA_Q_RANK, -1)], axis=1).astype(BF16)


def _pack_w_ukv(w):
    w = w.reshape(MLA_KV_RANK, MLA_H, MLA_NOPE + MLA_V)
    return jnp.concatenate([w[:, :, :MLA_NOPE].reshape(MLA_KV_RANK, -1),
                            w[:, :, MLA_NOPE:].reshape(MLA_KV_RANK, -1)], axis=1).astype(BF16)


def kernel(x_prompt, x_sample, state_gla, cache_mla, cache_diff_k, cache_diff_v, c, c_ctx,
           ada_w, ada_b, norm1_g, norm2_g, w_in, w_out, gla_gk_w, gla_gk_b, gla_onorm_g,
           mla_qnorm_g, mla_w_uq, mla_kvnorm_g, mla_w_ukv, mla_q_g, mla_k_g,
           diff_q_g, diff_k_g, diff_lambda, diff_onorm_g, pool_w, pool_scale,
           peer_wq, peer_keys, peer_u, peer_v):
    B, S, _ = x_prompt.shape
    DB, DS, _ = x_sample.shape
    L = ada_w.shape[0]
    t_ctx = B * S
    t_dec = DB * DS
    assert t_ctx % ROW_BLOCK == 0 and DS % ROW_BLOCK == 0 and t_ctx % DS == 0 and DB + 1 <= 16
    row = _mod_row_fn(t_ctx // ROW_BLOCK, DS // ROW_BLOCK)

    cond = jnp.concatenate([c_ctx[None, :], c, jnp.zeros((15 - DB, D), F32)], axis=0)
    mod = _ada_mod(cond, ada_w, ada_b)
    tables = _rope_tables(DS)
    kc_all = cache_diff_k.reshape(DB, L, -1, DIFF_H * LANES)
    vc_all = cache_diff_v.reshape(DB, L, -1, DIFF_H * LANES)

    state_gla_t = jnp.swapaxes(state_gla, -1, -2)
    mods = [mod[l].reshape(16, 1, 6 * D) for l in range(L)]
    x, h = _normmod_join(x_prompt.reshape(t_ctx, D), x_sample.reshape(t_dec, D),
                         norm1_g[0], mods[0], row)
    dec_blk0 = t_ctx // DS
    st_gla = jnp.zeros((B, L, 2, GLA_H, GLA_DV, GLA_DK), F32)
    st_mla = jnp.zeros((B, L, S, MLA_KV_RANK + MLA_ROPE), F32)
    st_dk = jnp.zeros((B, L, S, DIFF_H * LANES), F32)
    st_dv = jnp.zeros((B, L, S, DIFF_H * LANES), F32)
    for l in range(L):
        mod3 = mods[l]
        w_gla, w_mla, w_diff, w_pool = _pack_w_in(w_in[l])
        p_gla = _matmul(h, w_gla, name="in_gla")
        p_mla = _matmul(h, w_mla, name="in_mla")
        p_diff = _matmul(h, w_diff, name="in_diff")
        p_pool = _matmul(h, w_pool, name="in_pool")

        w_uq = _pack_w_uq(mla_w_uq[l])
        w_ukv = _pack_w_ukv(mla_w_ukv[l])
        gla_w = (gla_gk_w[l], gla_gk_b[l], gla_onorm_g[l])
        mla_w = (mla_qnorm_g[l], w_uq, mla_kvnorm_g[l], w_ukv, mla_q_g[l], mla_k_g[l])
        diff_w = (diff_q_g[l], diff_k_g[l], diff_lambda[l], diff_onorm_g[l])
        pool_wl = pool_w[l].astype(BF16)
        og_c, st_gla = _gla(p_gla, *gla_w, None, nb=B, S=S, blk0=0,
                            st_prev=st_gla, lidx=l, n_layers=L)
        og_d, _ = _gla(p_gla, *gla_w, state_gla_t, nb=DB, S=DS, blk0=dec_blk0,
                       st_prev=None, lidx=l, n_layers=L)
        om_c, st_mla = _mla(p_mla, *mla_w, None, None, nb=B, S=S, blk0=0, lidx=l,
                            new_prev=st_mla, n_layers=L)
        om_d, _ = _mla(p_mla, *mla_w, cache_mla, tables, nb=DB, S=DS, blk0=dec_blk0, lidx=l,
                       new_prev=None, n_layers=L)
        od_c, st_dk, st_dv = _diff(p_diff, *diff_w, None, None, None, nb=B, S=S, blk0=0, lidx=l,
                                   k_prev=st_dk, v_prev=st_dv, n_layers=L)
        od_d, _, _ = _diff(p_diff, *diff_w, kc_all, vc_all, tables, nb=DB, S=DS, blk0=dec_blk0,
                           lidx=l, k_prev=None, v_prev=None, n_layers=L)
        op_c = _pool(p_pool, pool_wl, pool_scale[l], nb=B, S=S, blk0=0)
        op_d = _pool(p_pool, pool_wl, pool_scale[l], nb=DB, S=DS, blk0=dec_blk0)

        x1, h2 = _outproj((og_c, om_c, od_c, op_c), (og_d, om_d, od_d, op_d),
                          w_out[l].astype(BF16), x, mod3, norm2_g[l], row)
        ii, jj, gg = _peer_topk(h2, peer_wq[l].astype(BF16), peer_keys[l].astype(BF16))
        ut = peer_u[l].reshape(PEER_EXPERTS // PEER_ET, PEER_ET, D).transpose(0, 2, 1).astype(BF16)
        peer = _peer_expert(h2, ii, jj, gg, ut, peer_v[l].astype(BF16))
        if l + 1 < L:
            x, h = _resid(x1, peer, mod3, row, (norm1_g[l + 1], mods[l + 1]))

    nc = t_ctx // ROW_BLOCK
    y_prompt, _ = _resid(x1, peer, mod3, row, blk0=0, nblk=nc)
    y_sample, _ = _resid(x1, peer, mod3, row, blk0=nc, nblk=t_dec // ROW_BLOCK)
    return (y_prompt.reshape(B, S, D), y_sample.reshape(DB, DS, D), jnp.swapaxes(st_gla, -1, -2), st_mla,
            st_dk.reshape(B, L, S, DIFF_H, 2 * DIFF_DQK), st_dv.reshape(B, L, S, DIFF_H, DIFF_DV))
```
